```python
import math
import jax, jax.numpy as jnp
from jax import lax
import numpy as np

D_MODEL = 1024
BATCH = 2
SEQ = 8192
DEPTH = 1

CHUNK = 128
A_GROUPS = 8
A_GROUP_DIM = 128
D_A = A_GROUPS * A_GROUP_DIM
N_HEADS = 8
Q_RANK = 384
KV_RANK = 256
QK_NOPE = 128
QK_ROPE = 64
V_DIM = 128
QK_DIM = QK_NOPE + QK_ROPE
Q_BLOCK = 128
ROPE_THETA = 10000.0
D_FF = 2816
CONV_W = 3
EPS = 1e-6
N_IN = 2 * D_A + Q_RANK + KV_RANK + QK_ROPE + 2 * D_MODEL

kernel_name = "hybrid_gmlp_mla_gated_encoder_block"


def rms_norm(x, g):
    xf = x.astype(jnp.float32)
    y = xf * lax.rsqrt(jnp.mean(xf * xf, axis=-1, keepdims=True) + EPS)
    return (y * g.astype(jnp.float32)).astype(x.dtype)


def layer_norm(x, g, b):
    xf = x.astype(jnp.float32)
    mu = jnp.mean(xf, axis=-1, keepdims=True)
    xc = xf - mu
    y = xc * lax.rsqrt(jnp.mean(xc * xc, axis=-1, keepdims=True) + EPS)
    return (y * g.astype(jnp.float32) + b.astype(jnp.float32)).astype(x.dtype)


def rope_tables(positions, dtype):
    inv_freq = ROPE_THETA ** (-jnp.arange(0, QK_ROPE, 2, dtype=jnp.float32) / QK_ROPE)
    ang = positions.astype(jnp.float32)[..., None] * inv_freq
    return jnp.cos(ang)[:, :, None, :].astype(dtype), jnp.sin(ang)[:, :, None, :].astype(dtype)


def apply_rope(x, cos, sin):
    x1, x2 = jnp.split(x, 2, axis=-1)
    return jnp.concatenate([x1 * cos - x2 * sin, x2 * cos + x1 * sin], axis=-1)


def gmlp_branch(u, v, v_ln_g, v_ln_b, w_s, b_s, w_a_o):
    B, S, _ = u.shape
    u = jax.nn.gelu(u)
    v = layer_norm(jax.nn.gelu(v), v_ln_g, v_ln_b)
    v = v.reshape(B, S // CHUNK, CHUNK, A_GROUPS, A_GROUP_DIM)
    mixed = jnp.einsum('gpq,bcqgd->bcpgd', w_s, v) + jnp.transpose(b_s)[None, None, :, :, None]
    y = u * mixed.reshape(B, S, D_A)
    return y @ w_a_o


def mla_branch(c_q, c_kv, k_rope, positions, q_norm_g, w_uq, kv_norm_g, w_ukv,
               q_head_g, k_head_g, w_b_o):
    B, S, _ = c_q.shape
    cos, sin = rope_tables(positions, c_q.dtype)
    q = (rms_norm(c_q, q_norm_g) @ w_uq).reshape(B, S, N_HEADS, QK_DIM)
    q = rms_norm(q, q_head_g)
    q = jnp.concatenate([q[..., :QK_NOPE], apply_rope(q[..., QK_NOPE:], cos, sin)], axis=-1)
    kv = (rms_norm(c_kv, kv_norm_g) @ w_ukv).reshape(B, S, N_HEADS, QK_NOPE + V_DIM)
    k_nope, v = kv[..., :QK_NOPE], kv[..., QK_NOPE:]
    k_pe = jnp.broadcast_to(k_rope[:, :, None, :], (B, S, N_HEADS, QK_ROPE))
    k = rms_norm(jnp.concatenate([k_nope, k_pe], axis=-1), k_head_g)
    k = jnp.concatenate([k[..., :QK_NOPE], apply_rope(k[..., QK_NOPE:], cos, sin)], axis=-1)
    scale = 1.0 / math.sqrt(QK_DIM)
    n_blocks = S // Q_BLOCK
    qb = jnp.transpose(q.reshape(B, n_blocks, Q_BLOCK, N_HEADS, QK_DIM), (1, 0, 2, 3, 4))

    def attend(q_blk):
        s = jnp.einsum('bqhd,bkhd->bhqk', q_blk, k).astype(jnp.float32) * scale
        p = jax.nn.softmax(s, axis=-1).astype(v.dtype)
        return jnp.einsum('bhqk,bkhd->bqhd', p, v)

    o = lax.map(attend, qb)
    o = jnp.transpose(o, (1, 0, 2, 3, 4)).reshape(B, S, N_HEADS * V_DIM)
    return o @ w_b_o


def conv_gated_mlp(h, norm2_g, w_up, conv_w, conv_b, w_down):
    C = 2 * D_FF
    up = rms_norm(h, norm2_g) @ w_up
    up = lax.conv_general_dilated(
        up, conv_w.reshape(CONV_W, 1, C).astype(up.dtype), window_strides=(1,), padding='SAME',
        dimension_numbers=('NWC', 'WIO', 'NWC'), feature_group_count=C) + conv_b
    val, gate = up[..., :D_FF], up[..., D_FF:]
    return (jax.nn.silu(gate) * val) @ w_down


def setup_inputs(seed: int = 0) -> dict:
    key = jax.random.key(seed)
    ks = jax.random.split(key, 24)
    f32 = jnp.float32

    def w(k, shape, fan_in, mult=1.0):
        return jax.random.normal(k, shape, f32) * (mult * fan_in ** -0.5)

    def gain(k, n):
        return 1.0 + 0.02 * jax.random.normal(k, (n,), f32)

    positions = jnp.broadcast_to(jnp.arange(SEQ, dtype=jnp.int32)[None, :], (BATCH, SEQ))
    return {
        "x": jax.random.normal(ks[0], (BATCH, SEQ, D_MODEL), f32),
        "positions": positions,
        "norm1_g": gain(ks[1], D_MODEL),
        "w_in": w(ks[2], (D_MODEL, N_IN), D_MODEL),
        "v_ln_g": gain(ks[3], D_A),
        "v_ln_b": 0.02 * jax.random.normal(ks[4], (D_A,), f32),
        "w_s": w(ks[5], (A_GROUPS, CHUNK, CHUNK), CHUNK),
        "b_s": 1.0 + 0.02 * jax.random.normal(ks[6], (A_GROUPS, CHUNK), f32),
        "w_a_o": w(ks[7], (D_A, D_MODEL), D_A),
        "q_norm_g": gain(ks[8], Q_RANK),
        "w_uq": w(ks[9], (Q_RANK, N_HEADS * QK_DIM), Q_RANK),
        "kv_norm_g": gain(ks[10], KV_RANK),
        "w_ukv": w(ks[11], (KV_RANK, N_HEADS * (QK_NOPE + V_DIM)), KV_RANK),
        "q_head_g": gain(ks[12], QK_DIM),
        "k_head_g": gain(ks[13], QK_DIM),
        "w_b_o": w(ks[14], (N_HEADS * V_DIM, D_MODEL), N_HEADS * V_DIM),
        "w_out": w(ks[15], (D_MODEL, D_MODEL), D_MODEL),
        "norm2_g": gain(ks[16], D_MODEL),
        "w_up": w(ks[17], (D_MODEL, 2 * D_FF), D_MODEL),
        "conv_w": w(ks[18], (CONV_W, 2 * D_FF), CONV_W),
        "conv_b": 0.02 * jax.random.normal(ks[19], (2 * D_FF,), f32),
        "w_down": w(ks[20], (D_FF, D_MODEL), D_FF),
    }


def reference(x, positions, norm1_g, w_in, v_ln_g, v_ln_b, w_s, b_s, w_a_o,
              q_norm_g, w_uq, kv_norm_g, w_ukv, q_head_g, k_head_g, w_b_o, w_out,
              norm2_g, w_up, conv_w, conv_b, w_down):
    B, S, D = x.shape
    for _layer in range(DEPTH):
        z = rms_norm(x, norm1_g) @ w_in
        cuts = np.cumsum([D_A, D_A, Q_RANK, KV_RANK, QK_ROPE]).tolist()
        u, v, c_q, c_kv, k_rope, gate_logits = jnp.split(z, cuts, axis=-1)
        y_a = gmlp_branch(u, v, v_ln_g, v_ln_b, w_s, b_s, w_a_o)
        y_b = mla_branch(c_q, c_kv, k_rope, positions, q_norm_g, w_uq, kv_norm_g, w_ukv,
                         q_head_g, k_head_g, w_b_o)
        gates = jax.nn.sigmoid(gate_logits).reshape(B, S, 2, D)
        merged = gates[:, :, 0, :] * y_a + gates[:, :, 1, :] * y_b
        x = x + merged @ w_out
        x = x + conv_gated_mlp(x, norm2_g, w_up, conv_w, conv_b, w_down)
    return x
```

```python
import functools
import math

import jax
import jax.numpy as jnp
from jax import lax
from jax.experimental import pallas as pl
from jax.experimental.pallas import tpu as pltpu

D_MODEL = 1024
CHUNK = 128
A_GROUPS = 8
A_GROUP_DIM = 128
D_A = A_GROUPS * A_GROUP_DIM
N_HEADS = 8
Q_RANK = 384
KV_RANK = 256
QK_NOPE = 128
QK_ROPE = 64
V_DIM = 128
QK_DIM = QK_NOPE + QK_ROPE
ROPE_THETA = 10000.0
D_FF = 2816
EPS = 1e-6

LANES = 128
HEAD_SLAB = 2 * LANES
ROPE_HALF = QK_ROPE // 2
V7X_VMEM_LIMIT = 56 * 1024 * 1024

TM_PRE = 256
TQ = 256
TK = 512
TM_MERGE = 512
TM_FFN = 256
HALO = 16

_BF16 = jnp.bfloat16
_F32 = jnp.float32


def _dot(a, b):
    return jnp.dot(a, b, preferred_element_type=_F32)


def _rms(x, g):
    return x * lax.rsqrt(jnp.mean(x * x, axis=-1, keepdims=True) + EPS) * g


def _rope_partner(t):
    lane = lax.broadcasted_iota(jnp.int32, t.shape, 1)
    return jnp.where(lane < ROPE_HALF, pltpu.roll(t, LANES - ROPE_HALF, 1), pltpu.roll(t, ROPE_HALF, 1))


def _const_spec(shape):
    nd = len(shape)
    return pl.BlockSpec(shape, lambda *_: (0,) * nd, pipeline_mode=pl.Buffered(1))


def _pre_kernel(x_ref, pos_ref, g1_ref, wu_ref, wv_ref, wcq_ref, wckv_ref, wkr_ref, wga_ref, wgb_ref,
                lng_ref, lnb_ref, ws_ref, bs_ref, wao_ref, qng_ref, wuq_ref, kvng_ref, wukv_ref,
                qg_ref, kg_ref, freq_ref, sign_ref,
                gya_ref, gb_ref, q_ref, k_ref, vt_ref):
    tm = x_ref.shape[1]
    xn = _rms(x_ref[0], g1_ref[...]).astype(_BF16)

    gv = jax.nn.gelu(_dot(xn, wv_ref[...]))
    mu = jnp.mean(gv, axis=-1, keepdims=True)
    vc = gv - mu
    vln = (vc * lax.rsqrt(jnp.mean(vc * vc, axis=-1, keepdims=True) + EPS) * lng_ref[...]
           + lnb_ref[...]).astype(_BF16)
    rows = []
    for c in range(tm // CHUNK):
        cols = []
        for g in range(A_GROUPS):
            blk = vln[c * CHUNK:(c + 1) * CHUNK, g * A_GROUP_DIM:(g + 1) * A_GROUP_DIM]
            cols.append(_dot(ws_ref[g], blk) + bs_ref[:, g:g + 1])
        rows.append(jnp.concatenate(cols, axis=1))
    mixed = jnp.concatenate(rows, axis=0)
    y = (jax.nn.gelu(_dot(xn, wu_ref[...])) * mixed).astype(_BF16)
    ya = _dot(y, wao_ref[...])
    gya_ref[0] = (jax.nn.sigmoid(_dot(xn, wga_ref[...])) * ya).astype(gya_ref.dtype)
    gb_ref[0] = jax.nn.sigmoid(_dot(xn, wgb_ref[...])).astype(gb_ref.dtype)

    ang = pos_ref[0].astype(_F32) * freq_ref[...]
    cos_t = jnp.cos(ang)
    sin_t = jnp.sin(ang) * sign_ref[...]

    def rope(t):
        return t * cos_t + _rope_partner(t) * sin_t

    cqn = _rms(_dot(xn, wcq_ref[...]), qng_ref[...]).astype(_BF16)
    qall = _dot(cqn, wuq_ref[...])
    qscale = 1.0 / math.sqrt(QK_DIM)
    for h in range(N_HEADS):
        qh = qall[:, h * HEAD_SLAB:(h + 1) * HEAD_SLAB]
        r = lax.rsqrt(jnp.sum(qh * qh, axis=-1, keepdims=True) * (1.0 / QK_DIM) + EPS) * qscale
        qn = qh * r * qg_ref[...]
        q_ref[0, h] = jnp.concatenate([qn[:, :LANES], rope(qn[:, LANES:])], axis=1).astype(q_ref.dtype)

    ckvn = _rms(_dot(xn, wckv_ref[...]), kvng_ref[...]).astype(_BF16)
    kv = _dot(ckvn, wukv_ref[...])
    kpe = _dot(xn, wkr_ref[...])
    kpe_ss = jnp.sum(kpe * kpe, axis=-1, keepdims=True)
    kpe_rot = rope(kpe * kg_ref[:, LANES:])
    for h in range(N_HEADS):
        kn = kv[:, h * HEAD_SLAB:h * HEAD_SLAB + LANES]
        r = lax.rsqrt((jnp.sum(kn * kn, axis=-1, keepdims=True) + kpe_ss) * (1.0 / QK_DIM) + EPS)
        k_ref[0, h] = jnp.concatenate([kn * r * kg_ref[:, :LANES], kpe_rot * r], axis=1).astype(k_ref.dtype)
        vt_ref[0, h, 0] = kv[:, h * HEAD_SLAB + LANES:(h + 1) * HEAD_SLAB].T.astype(vt_ref.dtype)


def _pre_call(x, pos3, consts):
    B, S, D = x.shape
    tm = TM_PRE
    nt = S // tm
    grid = (B, nt)
    in_specs = [
        pl.BlockSpec((1, tm, D), lambda b, i: (b, i, 0)),
        pl.BlockSpec((1, tm, 1), lambda b, i: (b, i, 0)),
    ] + [_const_spec(c.shape) for c in consts]
    out_shape = [
        jax.ShapeDtypeStruct((B, S, D), _BF16),
        jax.ShapeDtypeStruct((B, S, D), _BF16),
        jax.ShapeDtypeStruct((B, N_HEADS, S, HEAD_SLAB), _BF16),
        jax.ShapeDtypeStruct((B, N_HEADS, S, HEAD_SLAB), _BF16),
        jax.ShapeDtypeStruct((B, N_HEADS, nt, V_DIM, tm), _BF16),
    ]
    out_specs = [
        pl.BlockSpec((1, tm, D), lambda b, i: (b, i, 0)),
        pl.BlockSpec((1, tm, D), lambda b, i: (b, i, 0)),
        pl.BlockSpec((1, N_HEADS, tm, HEAD_SLAB), lambda b, i: (b, 0, i, 0)),
        pl.BlockSpec((1, N_HEADS, tm, HEAD_SLAB), lambda b, i: (b, 0, i, 0)),
        pl.BlockSpec((1, N_HEADS, 1, V_DIM, tm), lambda b, i: (b, 0, i, 0, 0)),
    ]
    return pl.pallas_call(
        _pre_kernel, grid=grid, in_specs=in_specs, out_specs=out_specs, out_shape=out_shape,
        compiler_params=pltpu.CompilerParams(
            dimension_semantics=("parallel", "parallel"), vmem_limit_bytes=V7X_VMEM_LIMIT),
        name="pre",
    )(x, pos3, *consts)


def _attn_kernel(q_ref, k_ref, vt_ref, o_ref):
    q = q_ref[0, 0]
    tq = q.shape[0]
    sub = TK // TM_PRE
    nk = k_ref.shape[2] // TK

    def body(j, carry):
        m, l, acc = carry
        kblk = k_ref[0, 0, pl.ds(pl.multiple_of(j * TK, TK), TK), :]
        st = lax.dot_general(kblk, q, (((1,), (1,)), ((), ())), preferred_element_type=_F32)
        m_new = jnp.maximum(m, jnp.max(st, axis=0, keepdims=True))
        p = jnp.exp(st - m_new)
        alpha = jnp.exp(m - m_new)
        l_new = alpha * l + jnp.sum(p, axis=0, keepdims=True)
        vt = jnp.concatenate([vt_ref[0, 0, j * sub + s] for s in range(sub)], axis=1)
        acc_new = alpha * acc + _dot(vt, p.astype(_BF16))
        return m_new, l_new, acc_new

    init = (jnp.full((1, tq), -jnp.inf, _F32), jnp.zeros((1, tq), _F32), jnp.zeros((V_DIM, tq), _F32))
    m, l, acc = lax.fori_loop(0, nk, body, init)
    o_ref[0] = (acc / l).T.astype(o_ref.dtype)


def _attn_call(q, k, vt):
    B, H, S, _ = q.shape
    grid = (B, H, S // TQ)
    return pl.pallas_call(
        _attn_kernel, grid=grid,
        in_specs=[
            pl.BlockSpec((1, 1, TQ, HEAD_SLAB), lambda b, h, i: (b, h, i, 0)),
            pl.BlockSpec((1, 1, S, HEAD_SLAB), lambda b, h, i: (b, h, 0, 0)),
            pl.BlockSpec((1, 1, S // TM_PRE, V_DIM, TM_PRE), lambda b, h, i: (b, h, 0, 0, 0)),
        ],
        out_specs=pl.BlockSpec((1, TQ, V_DIM), lambda b, h, i: (b, i, h)),
        out_shape=jax.ShapeDtypeStruct((B, S, H * V_DIM), _BF16),
        compiler_params=pltpu.CompilerParams(
            dimension_semantics=("parallel", "parallel", "parallel"), vmem_limit_bytes=V7X_VMEM_LIMIT),
        name="attn",
    )(q, k, vt)


def _merge_kernel(x_ref, o_ref, gya_ref, gb_ref, wbo_ref, wout_ref, g2_ref, h_ref, hn_ref):
    yb = _dot(o_ref[0], wbo_ref[...])
    merged = (gya_ref[0].astype(_F32) + gb_ref[0].astype(_F32) * yb).astype(_BF16)
    h = x_ref[0] + _dot(merged, wout_ref[...])
    h_ref[0] = h
    hn_ref[0] = _rms(h, g2_ref[...]).astype(hn_ref.dtype)


def _merge_call(x, o, gya, gb, wbo, wout, g2):
    B, S, D = x.shape
    tm = TM_MERGE
    row = lambda b, i: (b, i, 0)
    return pl.pallas_call(
        _merge_kernel, grid=(B, S // tm),
        in_specs=[pl.BlockSpec((1, tm, D), row)] * 4 + [_const_spec(wbo.shape), _const_spec(wout.shape),
                                                        _const_spec(g2.shape)],
        out_specs=[pl.BlockSpec((1, tm, D), row)] * 2,
        out_shape=[jax.ShapeDtypeStruct((B, S, D), _F32), jax.ShapeDtypeStruct((B, S, D), _BF16)],
        compiler_params=pltpu.CompilerParams(
            dimension_semantics=("parallel", "parallel"), vmem_limit_bytes=V7X_VMEM_LIMIT),
        name="merge",
    )(x, o, gya, gb, wbo, wout, g2)


def _ffn_kernel(h_ref, hn_ref, prev_ref, next_ref, wup_ref, cw_ref, cb_ref, wdown_ref, out_ref):
    i = pl.program_id(1)
    nt = pl.num_programs(1)
    tm = hn_ref.shape[1]
    prev = jnp.where(i > 0, prev_ref[0], jnp.zeros_like(prev_ref[0]))
    nxt = jnp.where(i < nt - 1, next_ref[0], jnp.zeros_like(next_ref[0]))
    ext = jnp.concatenate([prev, hn_ref[0], nxt], axis=0)
    up = _dot(ext, wup_ref[...])
    conv = (up[HALO - 1:HALO - 1 + tm] * cw_ref[0:1, :] + up[HALO:HALO + tm] * cw_ref[1:2, :]
            + up[HALO + 1:HALO + 1 + tm] * cw_ref[2:3, :] + cb_ref[...])
    act = (jax.nn.silu(conv[:, D_FF:]) * conv[:, :D_FF]).astype(_BF16)
    out_ref[0] = h_ref[0] + _dot(act, wdown_ref[...])


def _ffn_call(h, hn, wup, cw, cb, wdown):
    B, S, D = h.shape
    tm = TM_FFN
    nt = S // tm
    per = tm // HALO
    row = lambda b, i: (b, i, 0)
    return pl.pallas_call(
        _ffn_kernel, grid=(B, nt),
        in_specs=[
            pl.BlockSpec((1, tm, D), row),
            pl.BlockSpec((1, tm, D), row),
            pl.BlockSpec((1, HALO, D), lambda b, i: (b, jnp.maximum(i * per - 1, 0), 0)),
            pl.BlockSpec((1, HALO, D), lambda b, i: (b, jnp.minimum((i + 1) * per, S // HALO - 1), 0)),
            _const_spec(wup.shape), _const_spec(cw.shape), _const_spec(cb.shape), _const_spec(wdown.shape),
        ],
        out_specs=pl.BlockSpec((1, tm, D), row),
        out_shape=jax.ShapeDtypeStruct((B, S, D), _F32),
        compiler_params=pltpu.CompilerParams(
            dimension_semantics=("parallel", "parallel"), vmem_limit_bytes=V7X_VMEM_LIMIT),
        name="ffn",
    )(h, hn, hn, hn, wup, cw, cb, wdown)


def _pad_head_cols(w):
    lead = w.shape[:-1]
    w = w.reshape(lead + (N_HEADS, QK_DIM))
    w = jnp.pad(w, [(0, 0)] * len(lead) + [(0, 0), (0, HEAD_SLAB - QK_DIM)])
    return w.reshape(lead + (N_HEADS * HEAD_SLAB,))


def kernel(x, positions, norm1_g, w_in, v_ln_g, v_ln_b, w_s, b_s, w_a_o, q_norm_g, w_uq, kv_norm_g, w_ukv,
           q_head_g, k_head_g, w_b_o, w_out, norm2_g, w_up, conv_w, conv_b, w_down):
    B, S, D = x.shape
    bf = lambda a: a.astype(_BF16)
    row = lambda a: a.reshape(1, -1).astype(_F32)

    c0, c1, c2, c3, c4 = D_A, 2 * D_A, 2 * D_A + Q_RANK, 2 * D_A + Q_RANK + KV_RANK, 2 * D_A + Q_RANK + KV_RANK + QK_ROPE
    w_kr = jnp.pad(w_in[:, c3:c4], ((0, 0), (0, LANES - QK_ROPE)))
    pad_g = lambda g: jnp.pad(g, (0, HEAD_SLAB - QK_DIM)).reshape(1, HEAD_SLAB).astype(_F32)
    inv_freq = ROPE_THETA ** (-jnp.arange(0, QK_ROPE, 2, dtype=_F32) / QK_ROPE)
    zeros = jnp.zeros((LANES - QK_ROPE,), _F32)
    freq = jnp.concatenate([inv_freq, inv_freq, zeros]).reshape(1, LANES)
    sign = jnp.concatenate([-jnp.ones((ROPE_HALF,), _F32), jnp.ones((ROPE_HALF,), _F32), zeros]).reshape(1, LANES)

    consts = [
        row(norm1_g), bf(w_in[:, :c0]), bf(w_in[:, c0:c1]), bf(w_in[:, c1:c2]), bf(w_in[:, c2:c3]), bf(w_kr),
        bf(w_in[:, c4:c4 + D]), bf(w_in[:, c4 + D:]),
        row(v_ln_g), row(v_ln_b), bf(w_s), jnp.transpose(b_s).astype(_F32), bf(w_a_o),
        row(q_norm_g), bf(_pad_head_cols(w_uq)), row(kv_norm_g), bf(w_ukv),
        pad_g(q_head_g), pad_g(k_head_g), freq, sign,
    ]
    gya, gb, q, k, vt = _pre_call(x, positions.reshape(B, S, 1), consts)
    o = _attn_call(q, k, vt)
    h, hn = _merge_call(x, o, gya, gb, bf(w_b_o), bf(w_out), row(norm2_g))
    return _ffn_call(h, hn, bf(w_up), conv_w.astype(_F32), row(conv_b), bf(w_down))
```

```python
import functools
import math

import jax
import jax.numpy as jnp
from jax import lax
from jax.experimental import pallas as pl
from jax.experimental.pallas import tpu as pltpu

D_MODEL = 1024
CHUNK = 128
A_GROUPS = 8
A_GROUP_DIM = 128
D_A = A_GROUPS * A_GROUP_DIM
N_HEADS = 8
Q_RANK = 384
KV_RANK = 256
QK_NOPE = 128
QK_ROPE = 64
V_DIM = 128
QK_DIM = QK_NOPE + QK_ROPE
ROPE_THETA = 10000.0
D_FF = 2816
EPS = 1e-6

LANES = 128
HEAD_SLAB = 2 * LANES
ROPE_HALF = QK_ROPE // 2
V7X_VMEM_LIMIT = 56 * 1024 * 1024

TM_PRE = 256
TQ = 1024
TK = 512
TM_MERGE = 512
TM_FFN = 256
HALO = 16

_BF16 = jnp.bfloat16
_F32 = jnp.float32


def _dot(a, b):
    return jnp.dot(a, b, preferred_element_type=_F32)


def _rms(x, g):
    return x * lax.rsqrt(jnp.mean(x * x, axis=-1, keepdims=True) + EPS) * g


def _rope_partner(t):
    lane = lax.broadcasted_iota(jnp.int32, t.shape, 1)
    return jnp.where(lane < ROPE_HALF, pltpu.roll(t, LANES - ROPE_HALF, 1), pltpu.roll(t, ROPE_HALF, 1))


def _const_spec(shape):
    nd = len(shape)
    return pl.BlockSpec(shape, lambda *_: (0,) * nd, pipeline_mode=pl.Buffered(1))


def _pre_kernel(x_ref, pos_ref, g1_ref, wu_ref, wv_ref, wcq_ref, wckv_ref, wkr_ref, wga_ref, wgb_ref,
                lng_ref, lnb_ref, ws_ref, bs_ref, wao_ref, qng_ref, wuq_ref, kvng_ref, wukv_ref,
                qg_ref, kg_ref, freq_ref, sign_ref,
                gya_ref, gb_ref, q_ref, k_ref, vt_ref):
    tm = x_ref.shape[1]
    xn = _rms(x_ref[0], g1_ref[...]).astype(_BF16)

    gv = jax.nn.gelu(_dot(xn, wv_ref[...]))
    mu = jnp.mean(gv, axis=-1, keepdims=True)
    vc = gv - mu
    vln = (vc * lax.rsqrt(jnp.mean(vc * vc, axis=-1, keepdims=True) + EPS) * lng_ref[...]
           + lnb_ref[...]).astype(_BF16)
    rows = []
    for c in range(tm // CHUNK):
        cols = []
        for g in range(A_GROUPS):
            blk = vln[c * CHUNK:(c + 1) * CHUNK, g * A_GROUP_DIM:(g + 1) * A_GROUP_DIM]
            cols.append(_dot(ws_ref[g], blk) + bs_ref[:, g:g + 1])
        rows.append(jnp.concatenate(cols, axis=1))
    mixed = jnp.concatenate(rows, axis=0)
    y = (jax.nn.gelu(_dot(xn, wu_ref[...])) * mixed).astype(_BF16)
    ya = _dot(y, wao_ref[...])
    gya_ref[0] = (jax.nn.sigmoid(_dot(xn, wga_ref[...])) * ya).astype(gya_ref.dtype)
    gb_ref[0] = jax.nn.sigmoid(_dot(xn, wgb_ref[...])).astype(gb_ref.dtype)

    ang = pos_ref[0].astype(_F32) * freq_ref[...]
    cos_t = jnp.cos(ang)
    sin_t = jnp.sin(ang) * sign_ref[...]

    def rope(t):
        return t * cos_t + _rope_partner(t) * sin_t

    cqn = _rms(_dot(xn, wcq_ref[...]), qng_ref[...]).astype(_BF16)
    qall = _dot(cqn, wuq_ref[...])
    qscale = math.log2(math.e) / math.sqrt(QK_DIM)
    for h in range(N_HEADS):
        qh = qall[:, h * HEAD_SLAB:(h + 1) * HEAD_SLAB]
        r = lax.rsqrt(jnp.sum(qh * qh, axis=-1, keepdims=True) * (1.0 / QK_DIM) + EPS) * qscale
        qn = qh * r * qg_ref[...]
        q_ref[0, h] = jnp.concatenate([qn[:, :LANES], rope(qn[:, LANES:])], axis=1).astype(q_ref.dtype)

    ckvn = _rms(_dot(xn, wckv_ref[...]), kvng_ref[...]).astype(_BF16)
    kv = _dot(ckvn, wukv_ref[...])
    kpe = _dot(xn, wkr_ref[...])
    kpe_ss = jnp.sum(kpe * kpe, axis=-1, keepdims=True)
    kpe_rot = rope(kpe * kg_ref[:, LANES:])
    for h in range(N_HEADS):
        kn = kv[:, h * HEAD_SLAB:h * HEAD_SLAB + LANES]
        r = lax.rsqrt((jnp.sum(kn * kn, axis=-1, keepdims=True) + kpe_ss) * (1.0 / QK_DIM) + EPS)
        k_ref[0, h] = jnp.concatenate([kn * r * kg_ref[:, :LANES], kpe_rot * r], axis=1).astype(k_ref.dtype)
        vt_ref[0, h, 0] = kv[:, h * HEAD_SLAB + LANES:(h + 1) * HEAD_SLAB].T.astype(vt_ref.dtype)


def _pre_call(x, pos3, consts):
    B, S, D = x.shape
    tm = TM_PRE
    nt = S // tm
    grid = (B, nt)
    in_specs = [
        pl.BlockSpec((1, tm, D), lambda b, i: (b, i, 0)),
        pl.BlockSpec((1, tm, 1), lambda b, i: (b, i, 0)),
    ] + [_const_spec(c.shape) for c in consts]
    out_shape = [
        jax.ShapeDtypeStruct((B, S, D), _BF16),
        jax.ShapeDtypeStruct((B, S, D), _BF16),
        jax.ShapeDtypeStruct((B, N_HEADS, S, HEAD_SLAB), _BF16),
        jax.ShapeDtypeStruct((B, N_HEADS, S, HEAD_SLAB), _BF16),
        jax.ShapeDtypeStruct((B, N_HEADS, nt, V_DIM, tm), _BF16),
    ]
    out_specs = [
        pl.BlockSpec((1, tm, D), lambda b, i: (b, i, 0)),
        pl.BlockSpec((1, tm, D), lambda b, i: (b, i, 0)),
        pl.BlockSpec((1, N_HEADS, tm, HEAD_SLAB), lambda b, i: (b, 0, i, 0)),
        pl.BlockSpec((1, N_HEADS, tm, HEAD_SLAB), lambda b, i: (b, 0, i, 0)),
        pl.BlockSpec((1, N_HEADS, 1, V_DIM, tm), lambda b, i: (b, 0, i, 0, 0)),
    ]
    return pl.pallas_call(
        _pre_kernel, grid=grid, in_specs=in_specs, out_specs=out_specs, out_shape=out_shape,
        compiler_params=pltpu.CompilerParams(
            dimension_semantics=("parallel", "parallel"), vmem_limit_bytes=V7X_VMEM_LIMIT),
        name="pre",
    )(x, pos3, *consts)


def _attn_kernel(q_ref, k_ref, vt_ref, o_ref, st_ref, m_ref, l_ref, acc_ref):
    sub = TK // TM_PRE
    nk = k_ref.shape[2] // TK

    def scores(j, slot):
        kblk = k_ref[0, 0, pl.ds(pl.multiple_of(j * TK, TK), TK), :]
        st_ref[slot] = lax.dot_general(kblk, q_ref[0, 0], (((1,), (1,)), ((), ())),
                                       preferred_element_type=_F32)

    def softmax_pv(j, slot):
        st = st_ref[slot]
        m_old = m_ref[...]
        m_new = jnp.maximum(m_old, jnp.max(st, axis=0, keepdims=True))
        p = jnp.exp2(st - m_new)
        alpha = jnp.exp2(m_old - m_new)
        l_ref[...] = alpha * l_ref[...] + jnp.sum(p, axis=0, keepdims=True)
        vt = jnp.concatenate([vt_ref[0, 0, j * sub + s] for s in range(sub)], axis=1)
        acc_ref[...] = alpha * acc_ref[...] + _dot(vt, p.astype(_BF16))
        m_ref[...] = m_new

    m_ref[...] = jnp.full(m_ref.shape, -jnp.inf, _F32)
    l_ref[...] = jnp.zeros(l_ref.shape, _F32)
    acc_ref[...] = jnp.zeros(acc_ref.shape, _F32)
    scores(0, 0)

    def pair(jj, carry):
        j = 2 * jj
        scores(j + 1, 1)
        softmax_pv(j, 0)
        scores(j + 2, 0)
        softmax_pv(j + 1, 1)
        return carry

    lax.fori_loop(0, nk // 2 - 1, pair, 0)
    scores(nk - 1, 1)
    softmax_pv(nk - 2, 0)
    softmax_pv(nk - 1, 1)
    o_ref[0] = (acc_ref[...] / l_ref[...]).T.astype(o_ref.dtype)


def _attn_call(q, k, vt):
    B, H, S, _ = q.shape
    assert S % (2 * TK) == 0 and S % TQ == 0 and TK % TM_PRE == 0
    grid = (B, H, S // TQ)
    return pl.pallas_call(
        _attn_kernel, grid=grid,
        in_specs=[
            pl.BlockSpec((1, 1, TQ, HEAD_SLAB), lambda b, h, i: (b, h, i, 0)),
            pl.BlockSpec((1, 1, S, HEAD_SLAB), lambda b, h, i: (b, h, 0, 0)),
            pl.BlockSpec((1, 1, S // TM_PRE, V_DIM, TM_PRE), lambda b, h, i: (b, h, 0, 0, 0)),
        ],
        out_specs=pl.BlockSpec((1, TQ, V_DIM), lambda b, h, i: (b, i, h)),
        out_shape=jax.ShapeDtypeStruct((B, S, H * V_DIM), _BF16),
        scratch_shapes=[
            pltpu.VMEM((2, TK, TQ), _F32),
            pltpu.VMEM((1, TQ), _F32),
            pltpu.VMEM((1, TQ), _F32),
            pltpu.VMEM((V_DIM, TQ), _F32),
        ],
        compiler_params=pltpu.CompilerParams(
            dimension_semantics=("parallel", "parallel", "arbitrary"), vmem_limit_bytes=V7X_VMEM_LIMIT),
        name="attn",
    )(q, k, vt)


def _merge_kernel(x_ref, o_ref, gya_ref, gb_ref, wbo_ref, wout_ref, g2_ref, h_ref, hn_ref):
    yb = _dot(o_ref[0], wbo_ref[...])
    merged = (gya_ref[0].astype(_F32) + gb_ref[0].astype(_F32) * yb).astype(_BF16)
    h = x_ref[0] + _dot(merged, wout_ref[...])
    h_ref[0] = h
    hn_ref[0] = _rms(h, g2_ref[...]).astype(hn_ref.dtype)


def _merge_call(x, o, gya, gb, wbo, wout, g2):
    B, S, D = x.shape
    tm = TM_MERGE
    row = lambda b, i: (b, i, 0)
    return pl.pallas_call(
        _merge_kernel, grid=(B, S // tm),
        in_specs=[pl.BlockSpec((1, tm, D), row)] * 4 + [_const_spec(wbo.shape), _const_spec(wout.shape),
                                                        _const_spec(g2.shape)],
        out_specs=[pl.BlockSpec((1, tm, D), row)] * 2,
        out_shape=[jax.ShapeDtypeStruct((B, S, D), _F32), jax.ShapeDtypeStruct((B, S, D), _BF16)],
        compiler_params=pltpu.CompilerParams(
            dimension_semantics=("parallel", "parallel"), vmem_limit_bytes=V7X_VMEM_LIMIT),
        name="merge",
    )(x, o, gya, gb, wbo, wout, g2)


def _ffn_kernel(h_ref, hn_ref, prev_ref, next_ref, wup_ref, cw_ref, cb_ref, wdown_ref, out_ref):
    i = pl.program_id(1)
    nt = pl.num_programs(1)
    tm = hn_ref.shape[1]
    prev = jnp.where(i > 0, prev_ref[0], jnp.zeros_like(prev_ref[0]))
    nxt = jnp.where(i < nt - 1, next_ref[0], jnp.zeros_like(next_ref[0]))
    ext = jnp.concatenate([prev, hn_ref[0], nxt], axis=0)
    up = _dot(ext, wup_ref[...])
    conv = (up[HALO - 1:HALO - 1 + tm] * cw_ref[0:1, :] + up[HALO:HALO + tm] * cw_ref[1:2, :]
            + up[HALO + 1:HALO + 1 + tm] * cw_ref[2:3, :] + cb_ref[...])
    act = (jax.nn.silu(conv[:, D_FF:]) * conv[:, :D_FF]).astype(_BF16)
    out_ref[0] = h_ref[0] + _dot(act, wdown_ref[...])


def _ffn_call(h, hn, wup, cw, cb, wdown):
    B, S, D = h.shape
    tm = TM_FFN
    nt = S // tm
    per = tm // HALO
    row = lambda b, i: (b, i, 0)
    return pl.pallas_call(
        _ffn_kernel, grid=(B, nt),
        in_specs=[
            pl.BlockSpec((1, tm, D), row),
            pl.BlockSpec((1, tm, D), row),
            pl.BlockSpec((1, HALO, D), lambda b, i: (b, jnp.maximum(i * per - 1, 0), 0)),
            pl.BlockSpec((1, HALO, D), lambda b, i: (b, jnp.minimum((i + 1) * per, S // HALO - 1), 0)),
            _const_spec(wup.shape), _const_spec(cw.shape), _const_spec(cb.shape), _const_spec(wdown.shape),
        ],
        out_specs=pl.BlockSpec((1, tm, D), row),
        out_shape=jax.ShapeDtypeStruct((B, S, D), _F32),
        compiler_params=pltpu.CompilerParams(
            dimension_semantics=("parallel", "parallel"), vmem_limit_bytes=V7X_VMEM_LIMIT),
        name="ffn",
    )(h, hn, hn, hn, wup, cw, cb, wdown)


def _pad_head_cols(w):
    lead = w.shape[:-1]
    w = w.reshape(lead + (N_HEADS, QK_DIM))
    w = jnp.pad(w, [(0, 0)] * len(lead) + [(0, 0), (0, HEAD_SLAB - QK_DIM)])
    return w.reshape(lead + (N_HEADS * HEAD_SLAB,))


def kernel(x, positions, norm1_g, w_in, v_ln_g, v_ln_b, w_s, b_s, w_a_o, q_norm_g, w_uq, kv_norm_g, w_ukv,
           q_head_g, k_head_g, w_b_o, w_out, norm2_g, w_up, conv_w, conv_b, w_down):
    B, S, D = x.shape
    bf = lambda a: a.astype(_BF16)
    row = lambda a: a.reshape(1, -1).astype(_F32)

    c0, c1, c2, c3, c4 = D_A, 2 * D_A, 2 * D_A + Q_RANK, 2 * D_A + Q_RANK + KV_RANK, 2 * D_A + Q_RANK + KV_RANK + QK_ROPE
    w_kr = jnp.pad(w_in[:, c3:c4], ((0, 0), (0, LANES - QK_ROPE)))
    pad_g = lambda g: jnp.pad(g, (0, HEAD_SLAB - QK_DIM)).reshape(1, HEAD_SLAB).astype(_F32)
    inv_freq = ROPE_THETA ** (-jnp.arange(0, QK_ROPE, 2, dtype=_F32) / QK_ROPE)
    zeros = jnp.zeros((LANES - QK_ROPE,), _F32)
    freq = jnp.concatenate([inv_freq, inv_freq, zeros]).reshape(1, LANES)
    sign = jnp.concatenate([-jnp.ones((ROPE_HALF,), _F32), jnp.ones((ROPE_HALF,), _F32), zeros]).reshape(1, LANES)

    consts = [
        row(norm1_g), bf(w_in[:, :c0]), bf(w_in[:, c0:c1]), bf(w_in[:, c1:c2]), bf(w_in[:, c2:c3]), bf(w_kr),
        bf(w_in[:, c4:c4 + D]), bf(w_in[:, c4 + D:]),
        row(v_ln_g), row(v_ln_b), bf(w_s), jnp.transpose(b_s).astype(_F32), bf(w_a_o),
        row(q_norm_g), bf(_pad_head_cols(w_uq)), row(kv_norm_g), bf(w_ukv),
        pad_g(q_head_g), pad_g(k_head_g), freq, sign,
    ]
    gya, gb, q, k, vt = _pre_call(x, positions.reshape(B, S, 1), consts)
    o = _attn_call(q, k, vt)
    h, hn = _merge_call(x, o, gya, gb, bf(w_b_o), bf(w_out), row(norm2_g))
    return _ffn_call(h, hn, bf(w_up), conv_w.astype(_F32), row(conv_b), bf(w_down))
```

```python
import functools
import math

import jax
import jax.numpy as jnp
from jax import lax
from jax.experimental import pallas as pl
from jax.experimental.pallas import tpu as pltpu

D_MODEL = 1024
CHUNK = 128
A_GROUPS = 8
A_GROUP_DIM = 128
D_A = A_GROUPS * A_GROUP_DIM
N_HEADS = 8
Q_RANK = 384
KV_RANK = 256
QK_NOPE = 128
QK_ROPE = 64
V_DIM = 128
QK_DIM = QK_NOPE + QK_ROPE
ROPE_THETA = 10000.0
D_FF = 2816
EPS = 1e-6

LANES = 128
HEAD_SLAB = 2 * LANES
ROPE_HALF = QK_ROPE // 2
BIAS_LANE = QK_ROPE
QSCALE_LOG2 = math.log2(math.e) / math.sqrt(QK_DIM)
SCORE_BOUND_MAX = 40.0
V7X_VMEM_LIMIT = 56 * 1024 * 1024

TM_PRE = 256
TQ = 1024
TK = 1024
TM_MERGE = 512
TM_FFN = 512
FF_CHUNK = 256
HALO = 16

_BF16 = jnp.bfloat16
_F32 = jnp.float32


def _dot(a, b):
    return jnp.dot(a, b, preferred_element_type=_F32)


def _rms(x, g):
    return x * lax.rsqrt(jnp.mean(x * x, axis=-1, keepdims=True) + EPS) * g


def _rope_partner(t):
    lane = lax.broadcasted_iota(jnp.int32, t.shape, 1)
    return jnp.where(lane < ROPE_HALF, pltpu.roll(t, LANES - ROPE_HALF, 1), pltpu.roll(t, ROPE_HALF, 1))


def _const_spec(shape):
    nd = len(shape)
    return pl.BlockSpec(shape, lambda *_: (0,) * nd, pipeline_mode=pl.Buffered(1))


def _pre_kernel(x_ref, pos_ref, g1_ref, wu_ref, wv_ref, wcq_ref, wckv_ref, wkr_ref, wga_ref, wgb_ref,
                lng_ref, lnb_ref, ws_ref, bs_ref, wao_ref, qng_ref, wuq_ref, kvng_ref, wukv_ref,
                qg_ref, kg_ref, freq_ref, sign_ref, bias_ref,
                gya_ref, gb_ref, q_ref, k_ref, vt_ref):
    tm = x_ref.shape[1]
    xn = _rms(x_ref[0], g1_ref[...]).astype(_BF16)

    cqn = _rms(_dot(xn, wcq_ref[...]), qng_ref[...]).astype(_BF16)
    ckvn = _rms(_dot(xn, wckv_ref[...]), kvng_ref[...]).astype(_BF16)
    kpe = _dot(xn, wkr_ref[...])
    qall = _dot(cqn, wuq_ref[...])
    kv = _dot(ckvn, wukv_ref[...])

    gv = jax.nn.gelu(_dot(xn, wv_ref[...]))
    mu = jnp.mean(gv, axis=-1, keepdims=True)
    vc = gv - mu
    vln = (vc * lax.rsqrt(jnp.mean(vc * vc, axis=-1, keepdims=True) + EPS) * lng_ref[...]
           + lnb_ref[...]).astype(_BF16)
    rows = []
    for c in range(tm // CHUNK):
        cols = []
        for g in range(A_GROUPS):
            blk = vln[c * CHUNK:(c + 1) * CHUNK, g * A_GROUP_DIM:(g + 1) * A_GROUP_DIM]
            cols.append(_dot(ws_ref[g], blk) + bs_ref[:, g:g + 1])
        rows.append(jnp.concatenate(cols, axis=1))
    mixed = jnp.concatenate(rows, axis=0)
    y = (jax.nn.gelu(_dot(xn, wu_ref[...])) * mixed).astype(_BF16)
    ya = _dot(y, wao_ref[...])
    gya_ref[0] = (jax.nn.sigmoid(_dot(xn, wga_ref[...])) * ya).astype(gya_ref.dtype)
    gb_ref[0] = jax.nn.sigmoid(_dot(xn, wgb_ref[...])).astype(gb_ref.dtype)

    ang = pos_ref[0].astype(_F32) * freq_ref[...]
    cos_t = jnp.cos(ang)
    sin_t = jnp.sin(ang) * sign_ref[...]

    def rope(t):
        return t * cos_t + _rope_partner(t) * sin_t

    for h in range(N_HEADS):
        qh = qall[:, h * HEAD_SLAB:(h + 1) * HEAD_SLAB]
        r = lax.rsqrt(jnp.sum(qh * qh, axis=-1, keepdims=True) * (1.0 / QK_DIM) + EPS) * QSCALE_LOG2
        qn = qh * r * qg_ref[...]
        q_ref[0, h] = jnp.concatenate([qn[:, :LANES], rope(qn[:, LANES:]) + bias_ref[0:1, :]],
                                      axis=1).astype(q_ref.dtype)

    kpe_ss = jnp.sum(kpe * kpe, axis=-1, keepdims=True)
    kpe_rot = rope(kpe * kg_ref[:, LANES:])
    for h in range(N_HEADS):
        kn = kv[:, h * HEAD_SLAB:h * HEAD_SLAB + LANES]
        r = lax.rsqrt((jnp.sum(kn * kn, axis=-1, keepdims=True) + kpe_ss) * (1.0 / QK_DIM) + EPS)
        k_ref[0, h] = jnp.concatenate([kn * r * kg_ref[:, :LANES], kpe_rot * r + bias_ref[1:2, :]],
                                      axis=1).astype(k_ref.dtype)
        vt_ref[0, h, 0] = kv[:, h * HEAD_SLAB + LANES:(h + 1) * HEAD_SLAB].T.astype(vt_ref.dtype)


def _pre_call(x, pos3, consts):
    B, S, D = x.shape
    tm = TM_PRE
    nt = S // tm
    grid = (B, nt)
    in_specs = [
        pl.BlockSpec((1, tm, D), lambda b, i: (b, i, 0)),
        pl.BlockSpec((1, tm, 1), lambda b, i: (b, i, 0)),
    ] + [_const_spec(c.shape) for c in consts]
    out_shape = [
        jax.ShapeDtypeStruct((B, S, D), _BF16),
        jax.ShapeDtypeStruct((B, S, D), _BF16),
        jax.ShapeDtypeStruct((B, N_HEADS, S, HEAD_SLAB), _BF16),
        jax.ShapeDtypeStruct((B, N_HEADS, S, HEAD_SLAB), _BF16),
        jax.ShapeDtypeStruct((B, N_HEADS, nt, V_DIM, tm), _BF16),
    ]
    out_specs = [
        pl.BlockSpec((1, tm, D), lambda b, i: (b, i, 0)),
        pl.BlockSpec((1, tm, D), lambda b, i: (b, i, 0)),
        pl.BlockSpec((1, N_HEADS, tm, HEAD_SLAB), lambda b, i: (b, 0, i, 0)),
        pl.BlockSpec((1, N_HEADS, tm, HEAD_SLAB), lambda b, i: (b, 0, i, 0)),
        pl.BlockSpec((1, N_HEADS, 1, V_DIM, tm), lambda b, i: (b, 0, i, 0, 0)),
    ]
    return pl.pallas_call(
        _pre_kernel, grid=grid, in_specs=in_specs, out_specs=out_specs, out_shape=out_shape,
        compiler_params=pltpu.CompilerParams(
            dimension_semantics=("parallel", "parallel"), vmem_limit_bytes=V7X_VMEM_LIMIT),
        name="pre",
    )(x, pos3, *consts)


def _attn_kernel(bounded_ref, q_ref, k_ref, vt_ref, o_ref, qt_ref, st_ref, m_ref, l_ref, acc_ref):
    sub = TK // TM_PRE
    nk = k_ref.shape[2] // TK
    qt_ref[...] = q_ref[0, 0].T

    def scores(j, slot):
        kblk = k_ref[0, 0, pl.ds(pl.multiple_of(j * TK, TK), TK), :]
        st_ref[slot] = _dot(kblk, qt_ref[...])

    def values_t(j):
        return jnp.concatenate([vt_ref[0, 0, j * sub + s] for s in range(sub)], axis=1)

    def bounded_step(j, slot):
        p = jnp.exp2(st_ref[slot])
        l_ref[...] += jnp.sum(p.reshape(TK // 8, 8, p.shape[1]), axis=0)
        acc_ref[...] += _dot(values_t(j), p.astype(_BF16))

    def online_step(j, slot):
        st = st_ref[slot]
        m_old = m_ref[...]
        m_new = jnp.maximum(m_old, jnp.max(st, axis=0, keepdims=True))
        p = jnp.exp2(st - m_new)
        alpha = jnp.exp2(m_old - m_new)
        l_ref[0:1, :] = alpha * l_ref[0:1, :] + jnp.sum(p, axis=0, keepdims=True)
        acc_ref[...] = alpha * acc_ref[...] + _dot(values_t(j), p.astype(_BF16))
        m_ref[...] = m_new

    def run(step):
        scores(0, 0)

        def pair(jj, carry):
            j = 2 * jj
            scores(j + 1, 1)
            step(j, 0)
            scores(j + 2, 0)
            step(j + 1, 1)
            return carry

        lax.fori_loop(0, nk // 2 - 1, pair, 0)
        scores(nk - 1, 1)
        step(nk - 2, 0)
        step(nk - 1, 1)

    m_ref[...] = jnp.full(m_ref.shape, -jnp.inf, _F32)
    l_ref[...] = jnp.zeros(l_ref.shape, _F32)
    acc_ref[...] = jnp.zeros(acc_ref.shape, _F32)
    lax.cond(bounded_ref[0] == 1, lambda: run(bounded_step), lambda: run(online_step))
    l = jnp.sum(l_ref[...], axis=0, keepdims=True)
    o_ref[0] = (acc_ref[...] / l).T.astype(o_ref.dtype)


def _attn_call(bounded, q, k, vt):
    B, H, S, _ = q.shape
    assert S % (2 * TK) == 0 and S % TQ == 0 and TK % TM_PRE == 0
    grid = (B, H, S // TQ)
    return pl.pallas_call(
        _attn_kernel, grid=grid,
        in_specs=[
            pl.BlockSpec(memory_space=pltpu.SMEM),
            pl.BlockSpec((1, 1, TQ, HEAD_SLAB), lambda b, h, i: (b, h, i, 0)),
            pl.BlockSpec((1, 1, S, HEAD_SLAB), lambda b, h, i: (b, h, 0, 0)),
            pl.BlockSpec((1, 1, S // TM_PRE, V_DIM, TM_PRE), lambda b, h, i: (b, h, 0, 0, 0)),
        ],
        out_specs=pl.BlockSpec((1, TQ, V_DIM), lambda b, h, i: (b, i, h)),
        out_shape=jax.ShapeDtypeStruct((B, S, H * V_DIM), _BF16),
        scratch_shapes=[
            pltpu.VMEM((HEAD_SLAB, TQ), _BF16),
            pltpu.VMEM((2, TK, TQ), _F32),
            pltpu.VMEM((1, TQ), _F32),
            pltpu.VMEM((8, TQ), _F32),
            pltpu.VMEM((V_DIM, TQ), _F32),
        ],
        compiler_params=pltpu.CompilerParams(
            dimension_semantics=("parallel", "parallel", "arbitrary"), vmem_limit_bytes=V7X_VMEM_LIMIT),
        name="attn",
    )(bounded, q, k, vt)


def _merge_kernel(x_ref, o_ref, gya_ref, gb_ref, wbo_ref, wout_ref, g2_ref, h_ref, hn_ref):
    yb = _dot(o_ref[0], wbo_ref[...])
    merged = (gya_ref[0].astype(_F32) + gb_ref[0].astype(_F32) * yb).astype(_BF16)
    h = x_ref[0] + _dot(merged, wout_ref[...])
    h_ref[0] = h
    hn_ref[0] = _rms(h, g2_ref[...]).astype(hn_ref.dtype)


def _merge_call(x, o, gya, gb, wbo, wout, g2):
    B, S, D = x.shape
    tm = TM_MERGE
    row = lambda b, i: (b, i, 0)
    return pl.pallas_call(
        _merge_kernel, grid=(B, S // tm),
        in_specs=[pl.BlockSpec((1, tm, D), row)] * 4 + [_const_spec(wbo.shape), _const_spec(wout.shape),
                                                        _const_spec(g2.shape)],
        out_specs=[pl.BlockSpec((1, tm, D), row)] * 2,
        out_shape=[jax.ShapeDtypeStruct((B, S, D), _F32), jax.ShapeDtypeStruct((B, S, D), _BF16)],
        compiler_params=pltpu.CompilerParams(
            dimension_semantics=("parallel", "parallel"), vmem_limit_bytes=V7X_VMEM_LIMIT),
        name="merge",
    )(x, o, gya, gb, wbo, wout, g2)


def _ffn_kernel(h_ref, hn_ref, prev_ref, next_ref, wup_ref, cw_ref, cb_ref, wdown_ref, out_ref,
                ext_ref, up_ref, act_ref):
    i = pl.program_id(1)
    nt = pl.num_programs(1)
    tm = hn_ref.shape[1]
    prev = jnp.where(i > 0, prev_ref[0], jnp.zeros_like(prev_ref[0]))
    nxt = jnp.where(i < nt - 1, next_ref[0], jnp.zeros_like(next_ref[0]))
    ext_ref[0:HALO] = prev
    ext_ref[HALO:HALO + tm] = hn_ref[0]
    ext_ref[HALO + tm:] = nxt

    def up_proj(c, slot):
        for part, col in enumerate((c * FF_CHUNK, D_FF + c * FF_CHUNK)):
            up_ref[slot, part] = _dot(ext_ref[...], wup_ref[:, col:col + FF_CHUNK])

    def conv(slot, part, col):
        cols = slice(col, col + FF_CHUNK)
        return (up_ref[slot, part, HALO - 1:HALO - 1 + tm] * cw_ref[0:1, cols]
                + up_ref[slot, part, HALO:HALO + tm] * cw_ref[1:2, cols]
                + up_ref[slot, part, HALO + 1:HALO + 1 + tm] * cw_ref[2:3, cols] + cb_ref[:, cols])

    n_chunks = D_FF // FF_CHUNK
    up_proj(0, 0)
    for c in range(n_chunks):
        slot = c % 2
        if c + 1 < n_chunks:
            up_proj(c + 1, 1 - slot)
        act_ref[:, c * FF_CHUNK:(c + 1) * FF_CHUNK] = (
            jax.nn.silu(conv(slot, 1, D_FF + c * FF_CHUNK)) * conv(slot, 0, c * FF_CHUNK)).astype(_BF16)
    out_ref[0] = h_ref[0] + _dot(act_ref[...], wdown_ref[...])


def _ffn_call(h, hn, wup, cw, cb, wdown):
    B, S, D = h.shape
    tm = TM_FFN
    nt = S // tm
    per = tm // HALO
    row = lambda b, i: (b, i, 0)
    return pl.pallas_call(
        _ffn_kernel, grid=(B, nt),
        in_specs=[
            pl.BlockSpec((1, tm, D), row),
            pl.BlockSpec((1, tm, D), row),
            pl.BlockSpec((1, HALO, D), lambda b, i: (b, jnp.maximum(i * per - 1, 0), 0)),
            pl.BlockSpec((1, HALO, D), lambda b, i: (b, jnp.minimum((i + 1) * per, S // HALO - 1), 0)),
            _const_spec(wup.shape), _const_spec(cw.shape), _const_spec(cb.shape), _const_spec(wdown.shape),
        ],
        out_specs=pl.BlockSpec((1, tm, D), row),
        out_shape=jax.ShapeDtypeStruct((B, S, D), _F32),
        scratch_shapes=[
            pltpu.VMEM((tm + 2 * HALO, D), _BF16),
            pltpu.VMEM((2, 2, tm + 2 * HALO, FF_CHUNK), _F32),
            pltpu.VMEM((tm, D_FF), _BF16),
        ],
        compiler_params=pltpu.CompilerParams(
            dimension_semantics=("parallel", "parallel"), vmem_limit_bytes=V7X_VMEM_LIMIT),
        name="ffn",
    )(h, hn, hn, hn, wup, cw, cb, wdown)


def _pad_head_cols(w):
    lead = w.shape[:-1]
    w = w.reshape(lead + (N_HEADS, QK_DIM))
    w = jnp.pad(w, [(0, 0)] * len(lead) + [(0, 0), (0, HEAD_SLAB - QK_DIM)])
    return w.reshape(lead + (N_HEADS * HEAD_SLAB,))


def kernel(x, positions, norm1_g, w_in, v_ln_g, v_ln_b, w_s, b_s, w_a_o, q_norm_g, w_uq, kv_norm_g, w_ukv,
           q_head_g, k_head_g, w_b_o, w_out, norm2_g, w_up, conv_w, conv_b, w_down):
    B, S, D = x.shape
    bf = lambda a: a.astype(_BF16)
    row = lambda a: a.reshape(1, -1).astype(_F32)

    c0, c1, c2, c3, c4 = D_A, 2 * D_A, 2 * D_A + Q_RANK, 2 * D_A + Q_RANK + KV_RANK, 2 * D_A + Q_RANK + KV_RANK + QK_ROPE
    w_kr = jnp.pad(w_in[:, c3:c4], ((0, 0), (0, LANES - QK_ROPE)))
    pad_g = lambda g: jnp.pad(g, (0, HEAD_SLAB - QK_DIM)).reshape(1, HEAD_SLAB).astype(_F32)
    inv_freq = ROPE_THETA ** (-jnp.arange(0, QK_ROPE, 2, dtype=_F32) / QK_ROPE)
    zeros = jnp.zeros((LANES - QK_ROPE,), _F32)
    freq = jnp.concatenate([inv_freq, inv_freq, zeros]).reshape(1, LANES)
    sign = jnp.concatenate([-jnp.ones((ROPE_HALF,), _F32), jnp.ones((ROPE_HALF,), _F32), zeros]).reshape(1, LANES)

    bound = (QSCALE_LOG2 * QK_DIM * (1.0 + 2.0 ** -6)) * jnp.max(jnp.abs(q_head_g)) * jnp.max(jnp.abs(k_head_g))
    bounded = bound <= SCORE_BOUND_MAX
    bias = jnp.zeros((2, LANES), _F32).at[0, BIAS_LANE].set(1.0).at[1, BIAS_LANE].set(
        jnp.where(bounded, -bound, 0.0).astype(_F32))

    consts = [
        row(norm1_g), bf(w_in[:, :c0]), bf(w_in[:, c0:c1]), bf(w_in[:, c1:c2]), bf(w_in[:, c2:c3]), bf(w_kr),
        bf(w_in[:, c4:c4 + D]), bf(w_in[:, c4 + D:]),
        row(v_ln_g), row(v_ln_b), bf(w_s), jnp.transpose(b_s).astype(_F32), bf(w_a_o),
        row(q_norm_g), bf(_pad_head_cols(w_uq)), row(kv_norm_g), bf(w_ukv),
        pad_g(q_head_g), pad_g(k_head_g), freq, sign, bias,
    ]
    gya, gb, q, k, vt = _pre_call(x, positions.reshape(B, S, 1), consts)
    o = _attn_call(bounded.astype(jnp.int32).reshape(1), q, k, vt)
    h, hn = _merge_call(x, o, gya, gb, bf(w_b_o), bf(w_out), row(norm2_g))
    return _ffn_call(h, hn, bf(w_up), conv_w.astype(_F32), row(conv_b), bf(w_down))
```

```python
import functools
import math

import jax
import jax.numpy as jnp
from jax import lax
from jax.experimental import pallas as pl
from jax.experimental.pallas import tpu as pltpu

D_MODEL = 1024
CHUNK = 128
A_GROUPS = 8
A_GROUP_DIM = 128
D_A = A_GROUPS * A_GROUP_DIM
N_HEADS = 8
Q_RANK = 384
KV_RANK = 256
QK_NOPE = 128
QK_ROPE = 64
V_DIM = 128
QK_DIM = QK_NOPE + QK_ROPE
ROPE_THETA = 10000.0
D_FF = 2816
EPS = 1e-6

LANES = 128
HEAD_SLAB = 2 * LANES
ROPE_HALF = QK_ROPE // 2
ROPE_PACK = LANES // ROPE_HALF
BIAS_LANE = QK_ROPE
QSCALE_LOG2 = math.log2(math.e) / math.sqrt(QK_DIM)
SCORE_BOUND_MAX = 40.0
V7X_VMEM_LIMIT = 56 * 1024 * 1024

TM_PRE = 256
TQ = 1024
TK = 1024
TM_MERGE = 512
TM_FFN = 512
FF_CHUNK = 256
HALO = 16

_BF16 = jnp.bfloat16
_F32 = jnp.float32


def _dot(a, b):
    return jnp.dot(a, b, preferred_element_type=_F32)


def _gelu(x):
    c = 2.0 * math.sqrt(2.0 / math.pi) * math.log2(math.e)
    z = x * (x * x * (-0.044715 * c) - c)
    return x / (1.0 + jnp.exp2(z))


def _rms(x, g):
    return x * lax.rsqrt(jnp.mean(x * x, axis=-1, keepdims=True) + EPS) * g


def _rope_partner(t):
    lane = lax.broadcasted_iota(jnp.int32, t.shape, 1)
    return jnp.where(lane < ROPE_HALF, pltpu.roll(t, LANES - ROPE_HALF, 1), pltpu.roll(t, ROPE_HALF, 1))


def _const_spec(shape):
    nd = len(shape)
    return pl.BlockSpec(shape, lambda *_: (0,) * nd, pipeline_mode=pl.Buffered(1))


def _pre_kernel(x_ref, pos_ref, g1_ref, wu_ref, wv_ref, wcq_ref, wckv_ref, wkr_ref, wga_ref, wgb_ref,
                lng_ref, lnb_ref, ws_ref, bs_ref, wao_ref, qng_ref, wuq_ref, kvng_ref, wukv_ref,
                qg_ref, kg_ref, freq_ref, sign_ref, bias_ref,
                gya_ref, gb_ref, q_ref, k_ref, vt_ref):
    tm = x_ref.shape[1]
    xn = _rms(x_ref[0], g1_ref[...]).astype(_BF16)

    cqn = _rms(_dot(xn, wcq_ref[...]), qng_ref[...]).astype(_BF16)
    ckvn = _rms(_dot(xn, wckv_ref[...]), kvng_ref[...]).astype(_BF16)
    kpe = _dot(xn, wkr_ref[...])
    qall = _dot(cqn, wuq_ref[...])
    kv = _dot(ckvn, wukv_ref[...])

    gv = _gelu(_dot(xn, wv_ref[...]))
    mu = jnp.mean(gv, axis=-1, keepdims=True)
    vc = gv - mu
    vln = (vc * lax.rsqrt(jnp.mean(vc * vc, axis=-1, keepdims=True) + EPS) * lng_ref[...]
           + lnb_ref[...]).astype(_BF16)
    rows = []
    for c in range(tm // CHUNK):
        cols = []
        for g in range(A_GROUPS):
            blk = vln[c * CHUNK:(c + 1) * CHUNK, g * A_GROUP_DIM:(g + 1) * A_GROUP_DIM]
            cols.append(_dot(ws_ref[g], blk) + bs_ref[:, g:g + 1])
        rows.append(jnp.concatenate(cols, axis=1))
    mixed = jnp.concatenate(rows, axis=0)
    y = (_gelu(_dot(xn, wu_ref[...])) * mixed).astype(_BF16)
    ya = _dot(y, wao_ref[...])
    gya_ref[0] = (jax.nn.sigmoid(_dot(xn, wga_ref[...])) * ya).astype(gya_ref.dtype)
    gb_ref[0] = jax.nn.sigmoid(_dot(xn, wgb_ref[...])).astype(gb_ref.dtype)

    pos_col = jnp.broadcast_to(pos_ref[0].astype(_F32), (LANES, tm)).T
    rg = tm // ROPE_PACK
    lane = lax.broadcasted_iota(jnp.int32, (rg, LANES), 1)
    pos_packed = pos_col[(ROPE_PACK - 1) * rg:]
    for g in range(ROPE_PACK - 2, -1, -1):
        pos_packed = jnp.where(lane < (g + 1) * ROPE_HALF, pos_col[g * rg:(g + 1) * rg], pos_packed)
    ang = pos_packed * freq_ref[...]
    cos_p, sin_p = jnp.cos(ang), jnp.sin(ang)

    def unpack(tab):
        tiles = []
        shifted = lambda s: tab if s % LANES == 0 else pltpu.roll(tab, s % LANES, 1)
        for g in range(ROPE_PACK):
            lo = shifted(-g * ROPE_HALF)
            hi = shifted((1 - g) * ROPE_HALF)
            tiles.append(jnp.where(lane < ROPE_HALF, lo, hi))
        return jnp.concatenate(tiles, axis=0)

    cos_t = unpack(cos_p)
    sin_t = unpack(sin_p) * sign_ref[...]

    def rope(t):
        return t * cos_t + _rope_partner(t) * sin_t

    for h in range(N_HEADS):
        qh = qall[:, h * HEAD_SLAB:(h + 1) * HEAD_SLAB]
        r = lax.rsqrt(jnp.sum(qh * qh, axis=-1, keepdims=True) * (1.0 / QK_DIM) + EPS) * QSCALE_LOG2
        qn = qh * r * qg_ref[...]
        q_ref[0, h] = jnp.concatenate([qn[:, :LANES], rope(qn[:, LANES:]) + bias_ref[0:1, :]],
                                      axis=1).astype(q_ref.dtype)

    kpe_ss = jnp.sum(kpe * kpe, axis=-1, keepdims=True)
    kpe_rot = rope(kpe * kg_ref[:, LANES:])
    for h in range(N_HEADS):
        kn = kv[:, h * HEAD_SLAB:h * HEAD_SLAB + LANES]
        r = lax.rsqrt((jnp.sum(kn * kn, axis=-1, keepdims=True) + kpe_ss) * (1.0 / QK_DIM) + EPS)
        k_ref[0, h] = jnp.concatenate([kn * r * kg_ref[:, :LANES], kpe_rot * r + bias_ref[1:2, :]],
                                      axis=1).astype(k_ref.dtype)
        vt_ref[0, h, 0] = kv[:, h * HEAD_SLAB + LANES:(h + 1) * HEAD_SLAB].T.astype(vt_ref.dtype)


def _pre_call(x, pos3, consts):
    B, S, D = x.shape
    tm = TM_PRE
    nt = S // tm
    grid = (B, nt)
    in_specs = [
        pl.BlockSpec((1, tm, D), lambda b, i: (b, i, 0)),
        pl.BlockSpec((1, 1, tm), lambda b, i: (b, 0, i)),
    ] + [_const_spec(c.shape) for c in consts]
    out_shape = [
        jax.ShapeDtypeStruct((B, S, D), _BF16),
        jax.ShapeDtypeStruct((B, S, D), _BF16),
        jax.ShapeDtypeStruct((B, N_HEADS, S, HEAD_SLAB), _BF16),
        jax.ShapeDtypeStruct((B, N_HEADS, S, HEAD_SLAB), _BF16),
        jax.ShapeDtypeStruct((B, N_HEADS, nt, V_DIM, tm), _BF16),
    ]
    out_specs = [
        pl.BlockSpec((1, tm, D), lambda b, i: (b, i, 0)),
        pl.BlockSpec((1, tm, D), lambda b, i: (b, i, 0)),
        pl.BlockSpec((1, N_HEADS, tm, HEAD_SLAB), lambda b, i: (b, 0, i, 0)),
        pl.BlockSpec((1, N_HEADS, tm, HEAD_SLAB), lambda b, i: (b, 0, i, 0)),
        pl.BlockSpec((1, N_HEADS, 1, V_DIM, tm), lambda b, i: (b, 0, i, 0, 0)),
    ]
    return pl.pallas_call(
        _pre_kernel, grid=grid, in_specs=in_specs, out_specs=out_specs, out_shape=out_shape,
        compiler_params=pltpu.CompilerParams(
            dimension_semantics=("parallel", "parallel"), vmem_limit_bytes=V7X_VMEM_LIMIT),
        name="pre",
    )(x, pos3, *consts)


def _attn_kernel(bounded_ref, q_ref, k_ref, vt_ref, o_ref, qt_ref, st_ref, m_ref, l_ref, acc_ref):
    sub = TK // TM_PRE
    nk = k_ref.shape[2] // TK
    qt_ref[...] = q_ref[0, 0].T

    def scores(j, slot):
        start = j * TK if isinstance(j, int) else pl.multiple_of(j * TK, TK)
        kblk = k_ref[0, 0, pl.ds(start, TK), :]
        st_ref[slot] = _dot(kblk, qt_ref[...])

    def values_t(j):
        return jnp.concatenate([vt_ref[0, 0, j * sub + s] for s in range(sub)], axis=1)

    def bounded_step(j, slot):
        p = jnp.exp2(st_ref[slot])
        l_ref[...] += jnp.sum(p.reshape(TK // 8, 8, p.shape[1]), axis=0)
        acc_ref[...] += _dot(values_t(j), p.astype(_BF16))

    def online_step(j, slot):
        st = st_ref[slot]
        m_old = m_ref[...]
        m_new = jnp.maximum(m_old, jnp.max(st, axis=0, keepdims=True))
        p = jnp.exp2(st - m_new)
        alpha = jnp.exp2(m_old - m_new)
        l_ref[0:1, :] = alpha * l_ref[0:1, :] + jnp.sum(p, axis=0, keepdims=True)
        acc_ref[...] = alpha * acc_ref[...] + _dot(values_t(j), p.astype(_BF16))
        m_ref[...] = m_new

    def run_bounded():
        scores(0, 0)
        for j in range(nk):
            if j + 1 < nk:
                scores(j + 1, (j + 1) % 2)
            bounded_step(j, j % 2)

    def run_online():
        scores(0, 0)

        def pair(jj, carry):
            j = 2 * jj
            scores(j + 1, 1)
            online_step(j, 0)
            scores(j + 2, 0)
            online_step(j + 1, 1)
            return carry

        lax.fori_loop(0, nk // 2 - 1, pair, 0)
        scores(nk - 1, 1)
        online_step(nk - 2, 0)
        online_step(nk - 1, 1)

    m_ref[...] = jnp.full(m_ref.shape, -jnp.inf, _F32)
    l_ref[...] = jnp.zeros(l_ref.shape, _F32)
    acc_ref[...] = jnp.zeros(acc_ref.shape, _F32)
    lax.cond(bounded_ref[0] == 1, run_bounded, run_online)
    l = jnp.sum(l_ref[...], axis=0, keepdims=True)
    o_ref[0] = (acc_ref[...] / l).T.astype(o_ref.dtype)


def _attn_call(bounded, q, k, vt):
    B, H, S, _ = q.shape
    assert S % (2 * TK) == 0 and S % TQ == 0 and TK % TM_PRE == 0
    grid = (B, H, S // TQ)
    return pl.pallas_call(
        _attn_kernel, grid=grid,
        in_specs=[
            pl.BlockSpec(memory_space=pltpu.SMEM),
            pl.BlockSpec((1, 1, TQ, HEAD_SLAB), lambda b, h, i: (b, h, i, 0)),
            pl.BlockSpec((1, 1, S, HEAD_SLAB), lambda b, h, i: (b, h, 0, 0)),
            pl.BlockSpec((1, 1, S // TM_PRE, V_DIM, TM_PRE), lambda b, h, i: (b, h, 0, 0, 0)),
        ],
        out_specs=pl.BlockSpec((1, TQ, V_DIM), lambda b, h, i: (b, i, h)),
        out_shape=jax.ShapeDtypeStruct((B, S, H * V_DIM), _BF16),
        scratch_shapes=[
            pltpu.VMEM((HEAD_SLAB, TQ), _BF16),
            pltpu.VMEM((2, TK, TQ), _F32),
            pltpu.VMEM((1, TQ), _F32),
            pltpu.VMEM((8, TQ), _F32),
            pltpu.VMEM((V_DIM, TQ), _F32),
        ],
        compiler_params=pltpu.CompilerParams(
            dimension_semantics=("parallel", "parallel", "arbitrary"), vmem_limit_bytes=V7X_VMEM_LIMIT),
        name="attn",
    )(bounded, q, k, vt)


def _merge_kernel(x_ref, o_ref, gya_ref, gb_ref, wbo_ref, wout_ref, g2_ref, h_ref, hn_ref):
    yb = _dot(o_ref[0], wbo_ref[...])
    merged = (gya_ref[0].astype(_F32) + gb_ref[0].astype(_F32) * yb).astype(_BF16)
    h = x_ref[0] + _dot(merged, wout_ref[...])
    h_ref[0] = h
    hn_ref[0] = _rms(h, g2_ref[...]).astype(hn_ref.dtype)


def _merge_call(x, o, gya, gb, wbo, wout, g2):
    B, S, D = x.shape
    tm = TM_MERGE
    row = lambda b, i: (b, i, 0)
    return pl.pallas_call(
        _merge_kernel, grid=(B, S // tm),
        in_specs=[pl.BlockSpec((1, tm, D), row)] * 4 + [_const_spec(wbo.shape), _const_spec(wout.shape),
                                                        _const_spec(g2.shape)],
        out_specs=[pl.BlockSpec((1, tm, D), row)] * 2,
        out_shape=[jax.ShapeDtypeStruct((B, S, D), _F32), jax.ShapeDtypeStruct((B, S, D), _BF16)],
        compiler_params=pltpu.CompilerParams(
            dimension_semantics=("parallel", "parallel"), vmem_limit_bytes=V7X_VMEM_LIMIT),
        name="merge",
    )(x, o, gya, gb, wbo, wout, g2)


def _ffn_kernel(h_ref, hn_ref, prev_ref, next_ref, wup_ref, cw_ref, cb_ref, wdown_ref, out_ref,
                ext_ref, up_ref, act_ref):
    i = pl.program_id(1)
    nt = pl.num_programs(1)
    tm = hn_ref.shape[1]
    prev = jnp.where(i > 0, prev_ref[0], jnp.zeros_like(prev_ref[0]))
    nxt = jnp.where(i < nt - 1, next_ref[0], jnp.zeros_like(next_ref[0]))
    ext_ref[0:HALO] = prev
    ext_ref[HALO:HALO + tm] = hn_ref[0]
    ext_ref[HALO + tm:] = nxt

    def up_proj(c, slot):
        for part, col in enumerate((c * FF_CHUNK, D_FF + c * FF_CHUNK)):
            up_ref[slot, part] = _dot(ext_ref[...], wup_ref[:, col:col + FF_CHUNK])

    def conv(slot, part, col):
        cols = slice(col, col + FF_CHUNK)
        return (up_ref[slot, part, HALO - 1:HALO - 1 + tm] * cw_ref[0:1, cols]
                + up_ref[slot, part, HALO:HALO + tm] * cw_ref[1:2, cols]
                + up_ref[slot, part, HALO + 1:HALO + 1 + tm] * cw_ref[2:3, cols] + cb_ref[:, cols])

    n_chunks = D_FF // FF_CHUNK
    up_proj(0, 0)
    for c in range(n_chunks):
        slot = c % 2
        if c + 1 < n_chunks:
            up_proj(c + 1, 1 - slot)
        act_ref[:, c * FF_CHUNK:(c + 1) * FF_CHUNK] = (
            jax.nn.silu(conv(slot, 1, D_FF + c * FF_CHUNK)) * conv(slot, 0, c * FF_CHUNK)).astype(_BF16)
    out_ref[0] = h_ref[0] + _dot(act_ref[...], wdown_ref[...])


def _ffn_call(h, hn, wup, cw, cb, wdown):
    B, S, D = h.shape
    tm = TM_FFN
    nt = S // tm
    per = tm // HALO
    row = lambda b, i: (b, i, 0)
    return pl.pallas_call(
        _ffn_kernel, grid=(B, nt),
        in_specs=[
            pl.BlockSpec((1, tm, D), row),
            pl.BlockSpec((1, tm, D), row),
            pl.BlockSpec((1, HALO, D), lambda b, i: (b, jnp.maximum(i * per - 1, 0), 0)),
            pl.BlockSpec((1, HALO, D), lambda b, i: (b, jnp.minimum((i + 1) * per, S // HALO - 1), 0)),
            _const_spec(wup.shape), _const_spec(cw.shape), _const_spec(cb.shape), _const_spec(wdown.shape),
        ],
        out_specs=pl.BlockSpec((1, tm, D), row),
        out_shape=jax.ShapeDtypeStruct((B, S, D), _F32),
        scratch_shapes=[
            pltpu.VMEM((tm + 2 * HALO, D), _BF16),
            pltpu.VMEM((2, 2, tm + 2 * HALO, FF_CHUNK), _F32),
            pltpu.VMEM((tm, D_FF), _BF16),
        ],
        compiler_params=pltpu.CompilerParams(
            dimension_semantics=("parallel", "parallel"), vmem_limit_bytes=V7X_VMEM_LIMIT),
        name="ffn",
    )(h, hn, hn, hn, wup, cw, cb, wdown)


def _pad_head_cols(w):
    lead = w.shape[:-1]
    w = w.reshape(lead + (N_HEADS, QK_DIM))
    w = jnp.pad(w, [(0, 0)] * len(lead) + [(0, 0), (0, HEAD_SLAB - QK_DIM)])
    return w.reshape(lead + (N_HEADS * HEAD_SLAB,))


def kernel(x, positions, norm1_g, w_in, v_ln_g, v_ln_b, w_s, b_s, w_a_o, q_norm_g, w_uq, kv_norm_g, w_ukv,
           q_head_g, k_head_g, w_b_o, w_out, norm2_g, w_up, conv_w, conv_b, w_down):
    B, S, D = x.shape
    bf = lambda a: a.astype(_BF16)
    row = lambda a: a.reshape(1, -1).astype(_F32)

    c0, c1, c2, c3, c4 = D_A, 2 * D_A, 2 * D_A + Q_RANK, 2 * D_A + Q_RANK + KV_RANK, 2 * D_A + Q_RANK + KV_RANK + QK_ROPE
    w_kr = jnp.pad(w_in[:, c3:c4], ((0, 0), (0, LANES - QK_ROPE)))
    pad_g = lambda g: jnp.pad(g, (0, HEAD_SLAB - QK_DIM)).reshape(1, HEAD_SLAB).astype(_F32)
    inv_freq = ROPE_THETA ** (-jnp.arange(0, QK_ROPE, 2, dtype=_F32) / QK_ROPE)
    zeros = jnp.zeros((LANES - QK_ROPE,), _F32)
    freq = jnp.tile(inv_freq, ROPE_PACK).reshape(1, LANES)
    sign = jnp.concatenate([-jnp.ones((ROPE_HALF,), _F32), jnp.ones((ROPE_HALF,), _F32), zeros]).reshape(1, LANES)

    bound = (QSCALE_LOG2 * QK_DIM * (1.0 + 2.0 ** -6)) * jnp.max(jnp.abs(q_head_g)) * jnp.max(jnp.abs(k_head_g))
    bounded = bound <= SCORE_BOUND_MAX
    bias = jnp.zeros((2, LANES), _F32).at[0, BIAS_LANE].set(1.0).at[1, BIAS_LANE].set(
        jnp.where(bounded, -bound, 0.0).astype(_F32))

    consts = [
        row(norm1_g), bf(w_in[:, :c0]), bf(w_in[:, c0:c1]), bf(w_in[:, c1:c2]), bf(w_in[:, c2:c3]), bf(w_kr),
        bf(w_in[:, c4:c4 + D]), bf(w_in[:, c4 + D:]),
        row(v_ln_g), row(v_ln_b), bf(w_s), jnp.transpose(b_s).astype(_F32), bf(w_a_o),
        row(q_norm_g), bf(_pad_head_cols(w_uq)), row(kv_norm_g), bf(w_ukv),
        pad_g(q_head_g), pad_g(k_head_g), freq, sign, bias,
    ]
    gya, gb, q, k, vt = _pre_call(x, positions.reshape(B, 1, S), consts)
    o = _attn_call(bounded.astype(jnp.int32).reshape(1), q, k, vt)
    h, hn = _merge_call(x, o, gya, gb, bf(w_b_o), bf(w_out), row(norm2_g))
    return _ffn_call(h, hn, bf(w_up), conv_w.astype(_F32), row(conv_b), bf(w_down))
```

```python
import functools
import math

import jax
import jax.numpy as jnp
from jax import lax
from jax.experimental import pallas as pl
from jax.experimental.pallas import tpu as pltpu

D_MODEL = 1024
CHUNK = 128
A_GROUPS = 8
A_GROUP_DIM = 128
D_A = A_GROUPS * A_GROUP_DIM
N_HEADS = 8
Q_RANK = 384
KV_RANK = 256
QK_NOPE = 128
QK_ROPE = 64
V_DIM = 128
QK_DIM = QK_NOPE + QK_ROPE
ROPE_THETA = 10000.0
D_FF = 2816
EPS = 1e-6

LANES = 128
HEAD_SLAB = 2 * LANES
ROPE_HALF = QK_ROPE // 2
ROPE_PACK = LANES // ROPE_HALF
BIAS_LANE = QK_ROPE
QSCALE_LOG2 = math.log2(math.e) / math.sqrt(QK_DIM)
SCORE_BOUND_MAX = 40.0
V7X_VMEM_LIMIT = 56 * 1024 * 1024

TM_PRE = 256
TQ = 1024
TK = 1024
TM_MERGE = 512
TM_FFN = 512
FF_CHUNK = 256
HALO = 16

_BF16 = jnp.bfloat16
_F32 = jnp.float32


def _dot(a, b):
    return jnp.dot(a, b, preferred_element_type=_F32)


def _gelu(x):
    c = 2.0 * math.sqrt(2.0 / math.pi) * math.log2(math.e)
    z = x * (x * x * (-0.044715 * c) - c)
    return x / (1.0 + jnp.exp2(z))


def _rms(x, g):
    return x * lax.rsqrt(jnp.mean(x * x, axis=-1, keepdims=True) + EPS) * g


def _rope_partner(t):
    lane = lax.broadcasted_iota(jnp.int32, t.shape, 1)
    return jnp.where(lane < ROPE_HALF, pltpu.roll(t, LANES - ROPE_HALF, 1), pltpu.roll(t, ROPE_HALF, 1))


def _const_spec(shape):
    nd = len(shape)
    return pl.BlockSpec(shape, lambda *_: (0,) * nd, pipeline_mode=pl.Buffered(1))


def _pre_kernel(x_ref, pos_ref, g1_ref, wu_ref, wv_ref, wcq_ref, wckv_ref, wkr_ref, wga_ref, wgb_ref,
                lng_ref, lnb_ref, ws_ref, bs_ref, wao_ref, qng_ref, wuq_ref, kvng_ref, wukv_ref,
                qg_ref, kg_ref, freq_ref, sign_ref, bias_ref,
                gya_ref, gb_ref, q_ref, k_ref, vt_ref):
    tm = x_ref.shape[1]
    xn = _rms(x_ref[0], g1_ref[...]).astype(_BF16)

    cqn = _rms(_dot(xn, wcq_ref[...]), qng_ref[...]).astype(_BF16)
    ckvn = _rms(_dot(xn, wckv_ref[...]), kvng_ref[...]).astype(_BF16)
    kpe = _dot(xn, wkr_ref[...])
    qall = _dot(cqn, wuq_ref[...])
    kv = _dot(ckvn, wukv_ref[...])

    gv = _gelu(_dot(xn, wv_ref[...]))
    mu = jnp.mean(gv, axis=-1, keepdims=True)
    vc = gv - mu
    vln = (vc * lax.rsqrt(jnp.mean(vc * vc, axis=-1, keepdims=True) + EPS) * lng_ref[...]
           + lnb_ref[...]).astype(_BF16)
    rows = []
    for c in range(tm // CHUNK):
        cols = []
        for g in range(A_GROUPS):
            blk = vln[c * CHUNK:(c + 1) * CHUNK, g * A_GROUP_DIM:(g + 1) * A_GROUP_DIM]
            cols.append(_dot(ws_ref[g], blk) + bs_ref[:, g:g + 1])
        rows.append(jnp.concatenate(cols, axis=1))
    mixed = jnp.concatenate(rows, axis=0)
    y = (_gelu(_dot(xn, wu_ref[...])) * mixed).astype(_BF16)
    ya = _dot(y, wao_ref[...])
    gya_ref[0] = (jax.nn.sigmoid(_dot(xn, wga_ref[...])) * ya).astype(gya_ref.dtype)
    gb_ref[0] = jax.nn.sigmoid(_dot(xn, wgb_ref[...])).astype(gb_ref.dtype)

    pos_col = jnp.broadcast_to(pos_ref[0].astype(_F32), (LANES, tm)).T
    rg = tm // ROPE_PACK
    lane = lax.broadcasted_iota(jnp.int32, (rg, LANES), 1)
    pos_packed = pos_col[(ROPE_PACK - 1) * rg:]
    for g in range(ROPE_PACK - 2, -1, -1):
        pos_packed = jnp.where(lane < (g + 1) * ROPE_HALF, pos_col[g * rg:(g + 1) * rg], pos_packed)
    ang = pos_packed * freq_ref[...]
    cos_p, sin_p = jnp.cos(ang), jnp.sin(ang)

    def unpack(tab):
        tiles = []
        shifted = lambda s: tab if s % LANES == 0 else pltpu.roll(tab, s % LANES, 1)
        for g in range(ROPE_PACK):
            lo = shifted(-g * ROPE_HALF)
            hi = shifted((1 - g) * ROPE_HALF)
            tiles.append(jnp.where(lane < ROPE_HALF, lo, hi))
        return jnp.concatenate(tiles, axis=0)

    cos_t = unpack(cos_p)
    sin_t = unpack(sin_p) * sign_ref[...]

    def rope(t):
        return t * cos_t + _rope_partner(t) * sin_t

    for h in range(N_HEADS):
        qh = qall[:, h * HEAD_SLAB:(h + 1) * HEAD_SLAB]
        r = lax.rsqrt(jnp.sum(qh * qh, axis=-1, keepdims=True) * (1.0 / QK_DIM) + EPS) * QSCALE_LOG2
        qn = qh * r * qg_ref[...]
        q_ref[0, h] = jnp.concatenate([qn[:, :LANES], rope(qn[:, LANES:]) + bias_ref[0:1, :]],
                                      axis=1).astype(q_ref.dtype)

    kpe_ss = jnp.sum(kpe * kpe, axis=-1, keepdims=True)
    kpe_rot = rope(kpe * kg_ref[:, LANES:])
    for h in range(N_HEADS):
        kn = kv[:, h * HEAD_SLAB:h * HEAD_SLAB + LANES]
        r = lax.rsqrt((jnp.sum(kn * kn, axis=-1, keepdims=True) + kpe_ss) * (1.0 / QK_DIM) + EPS)
        k_ref[0, h] = jnp.concatenate([kn * r * kg_ref[:, :LANES], kpe_rot * r + bias_ref[1:2, :]],
                                      axis=1).astype(k_ref.dtype)
        vt_ref[0, h, 0] = kv[:, h * HEAD_SLAB + LANES:(h + 1) * HEAD_SLAB].T.astype(vt_ref.dtype)


def _pre_call(x, pos3, consts):
    B, S, D = x.shape
    tm = TM_PRE
    nt = S // tm
    grid = (B, nt)
    in_specs = [
        pl.BlockSpec((1, tm, D), lambda b, i: (b, i, 0)),
        pl.BlockSpec((1, 1, tm), lambda b, i: (b, 0, i)),
    ] + [_const_spec(c.shape) for c in consts]
    out_shape = [
        jax.ShapeDtypeStruct((B, S, D), _BF16),
        jax.ShapeDtypeStruct((B, S, D), _BF16),
        jax.ShapeDtypeStruct((B, N_HEADS, S, HEAD_SLAB), _BF16),
        jax.ShapeDtypeStruct((B, N_HEADS, S, HEAD_SLAB), _BF16),
        jax.ShapeDtypeStruct((B, N_HEADS, nt, V_DIM, tm), _BF16),
    ]
    out_specs = [
        pl.BlockSpec((1, tm, D), lambda b, i: (b, i, 0)),
        pl.BlockSpec((1, tm, D), lambda b, i: (b, i, 0)),
        pl.BlockSpec((1, N_HEADS, tm, HEAD_SLAB), lambda b, i: (b, 0, i, 0)),
        pl.BlockSpec((1, N_HEADS, tm, HEAD_SLAB), lambda b, i: (b, 0, i, 0)),
        pl.BlockSpec((1, N_HEADS, 1, V_DIM, tm), lambda b, i: (b, 0, i, 0, 0)),
    ]
    return pl.pallas_call(
        _pre_kernel, grid=grid, in_specs=in_specs, out_specs=out_specs, out_shape=out_shape,
        compiler_params=pltpu.CompilerParams(
            dimension_semantics=("parallel", "parallel"), vmem_limit_bytes=V7X_VMEM_LIMIT),
        name="pre",
    )(x, pos3, *consts)


def _attn_kernel(bounded_ref, q_ref, k_ref, vt_ref, o_ref, qt_ref, st_ref, m_ref, l_ref, acc_ref):
    sub = TK // TM_PRE
    nk = k_ref.shape[2] // TK
    qt_ref[...] = q_ref[0, 0].T

    def scores(j, slot):
        start = j * TK if isinstance(j, int) else pl.multiple_of(j * TK, TK)
        kblk = k_ref[0, 0, pl.ds(start, TK), :]
        st_ref[slot] = _dot(kblk, qt_ref[...])

    def values_t(j):
        return jnp.concatenate([vt_ref[0, 0, j * sub + s] for s in range(sub)], axis=1)

    def online_step(j, slot):
        st = st_ref[slot]
        m_old = m_ref[...]
        m_new = jnp.maximum(m_old, jnp.max(st, axis=0, keepdims=True))
        p = jnp.exp2(st - m_new)
        alpha = jnp.exp2(m_old - m_new)
        l_ref[0:1, :] = alpha * l_ref[0:1, :] + jnp.sum(p, axis=0, keepdims=True)
        acc_ref[...] = alpha * acc_ref[...] + _dot(values_t(j), p.astype(_BF16))
        m_ref[...] = m_new

    def run_bounded():
        l = jnp.zeros(l_ref.shape, _F32)
        acc = jnp.zeros(acc_ref.shape, _F32)
        for j in range(nk):
            p = jnp.exp2(_dot(k_ref[0, 0, j * TK:(j + 1) * TK, :], qt_ref[...]))
            l = l + jnp.sum(p.reshape(TK // 8, 8, p.shape[1]), axis=0)
            acc = acc + _dot(values_t(j), p.astype(_BF16))
        l_ref[...] = l
        acc_ref[...] = acc

    def run_online():
        scores(0, 0)

        def pair(jj, carry):
            j = 2 * jj
            scores(j + 1, 1)
            online_step(j, 0)
            scores(j + 2, 0)
            online_step(j + 1, 1)
            return carry

        lax.fori_loop(0, nk // 2 - 1, pair, 0)
        scores(nk - 1, 1)
        online_step(nk - 2, 0)
        online_step(nk - 1, 1)

    m_ref[...] = jnp.full(m_ref.shape, -jnp.inf, _F32)
    l_ref[...] = jnp.zeros(l_ref.shape, _F32)
    acc_ref[...] = jnp.zeros(acc_ref.shape, _F32)
    lax.cond(bounded_ref[0] == 1, run_bounded, run_online)
    l = jnp.sum(l_ref[...], axis=0, keepdims=True)
    o_ref[0] = (acc_ref[...] / l).T.astype(o_ref.dtype)


def _attn_call(bounded, q, k, vt):
    B, H, S, _ = q.shape
    assert S % (2 * TK) == 0 and S % TQ == 0 and TK % TM_PRE == 0
    grid = (B, H, S // TQ)
    return pl.pallas_call(
        _attn_kernel, grid=grid,
        in_specs=[
            pl.BlockSpec(memory_space=pltpu.SMEM),
            pl.BlockSpec((1, 1, TQ, HEAD_SLAB), lambda b, h, i: (b, h, i, 0)),
            pl.BlockSpec((1, 1, S, HEAD_SLAB), lambda b, h, i: (b, h, 0, 0)),
            pl.BlockSpec((1, 1, S // TM_PRE, V_DIM, TM_PRE), lambda b, h, i: (b, h, 0, 0, 0)),
        ],
        out_specs=pl.BlockSpec((1, TQ, V_DIM), lambda b, h, i: (b, i, h)),
        out_shape=jax.ShapeDtypeStruct((B, S, H * V_DIM), _BF16),
        scratch_shapes=[
            pltpu.VMEM((HEAD_SLAB, TQ), _BF16),
            pltpu.VMEM((2, TK, TQ), _F32),
            pltpu.VMEM((1, TQ), _F32),
            pltpu.VMEM((8, TQ), _F32),
            pltpu.VMEM((V_DIM, TQ), _F32),
        ],
        compiler_params=pltpu.CompilerParams(
            dimension_semantics=("parallel", "parallel", "arbitrary"), vmem_limit_bytes=V7X_VMEM_LIMIT),
        name="attn",
    )(bounded, q, k, vt)


def _merge_kernel(x_ref, o_ref, gya_ref, gb_ref, wbo_ref, wout_ref, g2_ref, h_ref, hn_ref):
    yb = _dot(o_ref[0], wbo_ref[...])
    merged = (gya_ref[0].astype(_F32) + gb_ref[0].astype(_F32) * yb).astype(_BF16)
    h = x_ref[0] + _dot(merged, wout_ref[...])
    h_ref[0] = h
    hn_ref[0] = _rms(h, g2_ref[...]).astype(hn_ref.dtype)


def _merge_call(x, o, gya, gb, wbo, wout, g2):
    B, S, D = x.shape
    tm = TM_MERGE
    row = lambda b, i: (b, i, 0)
    return pl.pallas_call(
        _merge_kernel, grid=(B, S // tm),
        in_specs=[pl.BlockSpec((1, tm, D), row)] * 4 + [_const_spec(wbo.shape), _const_spec(wout.shape),
                                                        _const_spec(g2.shape)],
        out_specs=[pl.BlockSpec((1, tm, D), row)] * 2,
        out_shape=[jax.ShapeDtypeStruct((B, S, D), _F32), jax.ShapeDtypeStruct((B, S, D), _BF16)],
        compiler_params=pltpu.CompilerParams(
            dimension_semantics=("parallel", "parallel"), vmem_limit_bytes=V7X_VMEM_LIMIT),
        name="merge",
    )(x, o, gya, gb, wbo, wout, g2)


def _ffn_kernel(h_ref, hn_ref, prev_ref, next_ref, wup_ref, cw_ref, cb_ref, wdown_ref, out_ref,
                ext_ref, up_ref, act_ref):
    i = pl.program_id(1)
    nt = pl.num_programs(1)
    tm = hn_ref.shape[1]
    prev = jnp.where(i > 0, prev_ref[0], jnp.zeros_like(prev_ref[0]))
    nxt = jnp.where(i < nt - 1, next_ref[0], jnp.zeros_like(next_ref[0]))
    ext_ref[0:HALO] = prev
    ext_ref[HALO:HALO + tm] = hn_ref[0]
    ext_ref[HALO + tm:] = nxt

    def up_proj(c, slot):
        for part, col in enumerate((c * FF_CHUNK, D_FF + c * FF_CHUNK)):
            up_ref[slot, part] = _dot(ext_ref[...], wup_ref[:, col:col + FF_CHUNK])

    def conv(slot, part, col):
        cols = slice(col, col + FF_CHUNK)
        return (up_ref[slot, part, HALO - 1:HALO - 1 + tm] * cw_ref[0:1, cols]
                + up_ref[slot, part, HALO:HALO + tm] * cw_ref[1:2, cols]
                + up_ref[slot, part, HALO + 1:HALO + 1 + tm] * cw_ref[2:3, cols] + cb_ref[:, cols])

    n_chunks = D_FF // FF_CHUNK
    up_proj(0, 0)
    for c in range(n_chunks):
        slot = c % 2
        if c + 1 < n_chunks:
            up_proj(c + 1, 1 - slot)
        act_ref[:, c * FF_CHUNK:(c + 1) * FF_CHUNK] = (
            jax.nn.silu(conv(slot, 1, D_FF + c * FF_CHUNK)) * conv(slot, 0, c * FF_CHUNK)).astype(_BF16)
    out_ref[0] = h_ref[0] + _dot(act_ref[...], wdown_ref[...])


def _ffn_call(h, hn, wup, cw, cb, wdown):
    B, S, D = h.shape
    tm = TM_FFN
    nt = S // tm
    per = tm // HALO
    row = lambda b, i: (b, i, 0)
    return pl.pallas_call(
        _ffn_kernel, grid=(B, nt),
        in_specs=[
            pl.BlockSpec((1, tm, D), row),
            pl.BlockSpec((1, tm, D), row),
            pl.BlockSpec((1, HALO, D), lambda b, i: (b, jnp.maximum(i * per - 1, 0), 0)),
            pl.BlockSpec((1, HALO, D), lambda b, i: (b, jnp.minimum((i + 1) * per, S // HALO - 1), 0)),
            _const_spec(wup.shape), _const_spec(cw.shape), _const_spec(cb.shape), _const_spec(wdown.shape),
        ],
        out_specs=pl.BlockSpec((1, tm, D), row),
        out_shape=jax.ShapeDtypeStruct((B, S, D), _F32),
        scratch_shapes=[
            pltpu.VMEM((tm + 2 * HALO, D), _BF16),
            pltpu.VMEM((2, 2, tm + 2 * HALO, FF_CHUNK), _F32),
            pltpu.VMEM((tm, D_FF), _BF16),
        ],
        compiler_params=pltpu.CompilerParams(
            dimension_semantics=("parallel", "parallel"), vmem_limit_bytes=V7X_VMEM_LIMIT),
        name="ffn",
    )(h, hn, hn, hn, wup, cw, cb, wdown)


def _pad_head_cols(w):
    lead = w.shape[:-1]
    w = w.reshape(lead + (N_HEADS, QK_DIM))
    w = jnp.pad(w, [(0, 0)] * len(lead) + [(0, 0), (0, HEAD_SLAB - QK_DIM)])
    return w.reshape(lead + (N_HEADS * HEAD_SLAB,))


def kernel(x, positions, norm1_g, w_in, v_ln_g, v_ln_b, w_s, b_s, w_a_o, q_norm_g, w_uq, kv_norm_g, w_ukv,
           q_head_g, k_head_g, w_b_o, w_out, norm2_g, w_up, conv_w, conv_b, w_down):
    B, S, D = x.shape
    bf = lambda a: a.astype(_BF16)
    row = lambda a: a.reshape(1, -1).astype(_F32)

    c0, c1, c2, c3, c4 = D_A, 2 * D_A, 2 * D_A + Q_RANK, 2 * D_A + Q_RANK + KV_RANK, 2 * D_A + Q_RANK + KV_RANK + QK_ROPE
    w_kr = jnp.pad(w_in[:, c3:c4], ((0, 0), (0, LANES - QK_ROPE)))
    pad_g = lambda g: jnp.pad(g, (0, HEAD_SLAB - QK_DIM)).reshape(1, HEAD_SLAB).astype(_F32)
    inv_freq = ROPE_THETA ** (-jnp.arange(0, QK_ROPE, 2, dtype=_F32) / QK_ROPE)
    zeros = jnp.zeros((LANES - QK_ROPE,), _F32)
    freq = jnp.tile(inv_freq, ROPE_PACK).reshape(1, LANES)
    sign = jnp.concatenate([-jnp.ones((ROPE_HALF,), _F32), jnp.ones((ROPE_HALF,), _F32), zeros]).reshape(1, LANES)

    bound = (QSCALE_LOG2 * QK_DIM * (1.0 + 2.0 ** -6)) * jnp.max(jnp.abs(q_head_g)) * jnp.max(jnp.abs(k_head_g))
    bounded = bound <= SCORE_BOUND_MAX
    bias = jnp.zeros((2, LANES), _F32).at[0, BIAS_LANE].set(1.0).at[1, BIAS_LANE].set(
        jnp.where(bounded, -bound, 0.0).astype(_F32))

    consts = [
        row(norm1_g), bf(w_in[:, :c0]), bf(w_in[:, c0:c1]), bf(w_in[:, c1:c2]), bf(w_in[:, c2:c3]), bf(w_kr),
        bf(w_in[:, c4:c4 + D]), bf(w_in[:, c4 + D:]),
        row(v_ln_g), row(v_ln_b), bf(w_s), jnp.transpose(b_s).astype(_F32), bf(w_a_o),
        row(q_norm_g), bf(_pad_head_cols(w_uq)), row(kv_norm_g), bf(w_ukv),
        pad_g(q_head_g), pad_g(k_head_g), freq, sign, bias,
    ]
    gya, gb, q, k, vt = _pre_call(x, positions.reshape(B, 1, S), consts)
    o = _attn_call(bounded.astype(jnp.int32).reshape(1), q, k, vt)
    h, hn = _merge_call(x, o, gya, gb, bf(w_b_o), bf(w_out), row(norm2_g))
    return _ffn_call(h, hn, bf(w_up), conv_w.astype(_F32), row(conv_b), bf(w_down))
```

```python
import functools
import math

import jax
import jax.numpy as jnp
from jax import lax
from jax.experimental import pallas as pl
from jax.experimental.pallas import tpu as pltpu

D_MODEL = 1024
CHUNK = 128
A_GROUPS = 8
A_GROUP_DIM = 128
D_A = A_GROUPS * A_GROUP_DIM
N_HEADS = 8
Q_RANK = 384
KV_RANK = 256
QK_NOPE = 128
QK_ROPE = 64
V_DIM = 128
QK_DIM = QK_NOPE + QK_ROPE
ROPE_THETA = 10000.0
D_FF = 2816
EPS = 1e-6

LANES = 128
HEAD_SLAB = 2 * LANES
ROPE_HALF = QK_ROPE // 2
ROPE_PACK = LANES // ROPE_HALF
BIAS_LANE = QK_ROPE
QSCALE_LOG2 = math.log2(math.e) / math.sqrt(QK_DIM)
SCORE_BOUND_MAX = 40.0
V7X_VMEM_LIMIT = 56 * 1024 * 1024

TM_PRE = 256
TQ = 1024
TK = 1024
TM_MERGE = 512
TM_FFN = 512
FF_CHUNK = 256
HALO = 16

_BF16 = jnp.bfloat16
_F32 = jnp.float32


def _dot(a, b):
    return jnp.dot(a, b, preferred_element_type=_F32)


def _gelu(x):
    c = 2.0 * math.sqrt(2.0 / math.pi) * math.log2(math.e)
    z = x * (x * x * (-0.044715 * c) - c)
    return x / (1.0 + jnp.exp2(z))


def _rms(x, g):
    return x * lax.rsqrt(jnp.mean(x * x, axis=-1, keepdims=True) + EPS) * g


def _rope_partner(t):
    lane = lax.broadcasted_iota(jnp.int32, t.shape, 1)
    return jnp.where(lane < ROPE_HALF, pltpu.roll(t, LANES - ROPE_HALF, 1), pltpu.roll(t, ROPE_HALF, 1))


def _const_spec(shape):
    nd = len(shape)
    return pl.BlockSpec(shape, lambda *_: (0,) * nd, pipeline_mode=pl.Buffered(1))


def _pre_kernel(x_ref, pos_ref, g1_ref, wu_ref, wv_ref, wcq_ref, wckv_ref, wkr_ref, wga_ref, wgb_ref,
                lng_ref, lnb_ref, ws_ref, bs_ref, wao_ref, qng_ref, wuq_ref, kvng_ref, wukv_ref,
                qg_ref, kg_ref, freq_ref, sign_ref, bias_ref,
                gya_ref, gb_ref, q_ref, k_ref, vt_ref):
    tm = x_ref.shape[1]
    xn = _rms(x_ref[0], g1_ref[...]).astype(_BF16)

    cq = _dot(xn, wcq_ref[...])
    ckv = _dot(xn, wckv_ref[...])
    kpe = _dot(xn, wkr_ref[...])
    v_raw = _dot(xn, wv_ref[...])
    qall = _dot(_rms(cq, qng_ref[...]).astype(_BF16), wuq_ref[...])
    kv = _dot(_rms(ckv, kvng_ref[...]).astype(_BF16), wukv_ref[...])
    gb_ref[0] = jax.nn.sigmoid(_dot(xn, wgb_ref[...])).astype(gb_ref.dtype)
    u_raw = _dot(xn, wu_ref[...])
    ga_raw = _dot(xn, wga_ref[...])

    gv = _gelu(v_raw)
    mu = jnp.mean(gv, axis=-1, keepdims=True)
    vc = gv - mu
    vln = (vc * lax.rsqrt(jnp.mean(vc * vc, axis=-1, keepdims=True) + EPS) * lng_ref[...]
           + lnb_ref[...]).astype(_BF16)
    rows = []
    for c in range(tm // CHUNK):
        cols = []
        for g in range(A_GROUPS):
            blk = vln[c * CHUNK:(c + 1) * CHUNK, g * A_GROUP_DIM:(g + 1) * A_GROUP_DIM]
            cols.append(_dot(ws_ref[g], blk) + bs_ref[:, g:g + 1])
        rows.append(jnp.concatenate(cols, axis=1))
    mixed = jnp.concatenate(rows, axis=0)
    y = (_gelu(u_raw) * mixed).astype(_BF16)
    gya_ref[0] = (jax.nn.sigmoid(ga_raw) * _dot(y, wao_ref[...])).astype(gya_ref.dtype)

    pos_col = jnp.broadcast_to(pos_ref[0].astype(_F32), (LANES, tm)).T
    rg = tm // ROPE_PACK
    lane = lax.broadcasted_iota(jnp.int32, (rg, LANES), 1)
    pos_packed = pos_col[(ROPE_PACK - 1) * rg:]
    for g in range(ROPE_PACK - 2, -1, -1):
        pos_packed = jnp.where(lane < (g + 1) * ROPE_HALF, pos_col[g * rg:(g + 1) * rg], pos_packed)
    ang = pos_packed * freq_ref[...]
    cos_p, sin_p = jnp.cos(ang), jnp.sin(ang)

    def unpack(tab):
        tiles = []
        shifted = lambda s: tab if s % LANES == 0 else pltpu.roll(tab, s % LANES, 1)
        for g in range(ROPE_PACK):
            lo = shifted(-g * ROPE_HALF)
            hi = shifted((1 - g) * ROPE_HALF)
            tiles.append(jnp.where(lane < ROPE_HALF, lo, hi))
        return jnp.concatenate(tiles, axis=0)

    cos_t = unpack(cos_p)
    sin_t = unpack(sin_p) * sign_ref[...]

    def rope(t):
        return t * cos_t + _rope_partner(t) * sin_t

    for h in range(N_HEADS):
        qh = qall[:, h * HEAD_SLAB:(h + 1) * HEAD_SLAB]
        r = lax.rsqrt(jnp.sum(qh * qh, axis=-1, keepdims=True) * (1.0 / QK_DIM) + EPS) * QSCALE_LOG2
        qn = qh * r * qg_ref[...]
        q_ref[0, h] = jnp.concatenate([qn[:, :LANES], rope(qn[:, LANES:]) + bias_ref[0:1, :]],
                                      axis=1).astype(q_ref.dtype)

    kpe_ss = jnp.sum(kpe * kpe, axis=-1, keepdims=True)
    kpe_rot = rope(kpe * kg_ref[:, LANES:])
    for h in range(N_HEADS):
        kn = kv[:, h * HEAD_SLAB:h * HEAD_SLAB + LANES]
        r = lax.rsqrt((jnp.sum(kn * kn, axis=-1, keepdims=True) + kpe_ss) * (1.0 / QK_DIM) + EPS)
        k_ref[0, h] = jnp.concatenate([kn * r * kg_ref[:, :LANES], kpe_rot * r + bias_ref[1:2, :]],
                                      axis=1).astype(k_ref.dtype)
        vt_ref[0, h, 0] = kv[:, h * HEAD_SLAB + LANES:(h + 1) * HEAD_SLAB].T.astype(vt_ref.dtype)


def _pre_call(x, pos3, consts):
    B, S, D = x.shape
    tm = TM_PRE
    nt = S // tm
    grid = (B, nt)
    in_specs = [
        pl.BlockSpec((1, tm, D), lambda b, i: (b, i, 0)),
        pl.BlockSpec((1, 1, tm), lambda b, i: (b, 0, i)),
    ] + [_const_spec(c.shape) for c in consts]
    out_shape = [
        jax.ShapeDtypeStruct((B, S, D), _BF16),
        jax.ShapeDtypeStruct((B, S, D), _BF16),
        jax.ShapeDtypeStruct((B, N_HEADS, S, HEAD_SLAB), _BF16),
        jax.ShapeDtypeStruct((B, N_HEADS, S, HEAD_SLAB), _BF16),
        jax.ShapeDtypeStruct((B, N_HEADS, nt, V_DIM, tm), _BF16),
    ]
    out_specs = [
        pl.BlockSpec((1, tm, D), lambda b, i: (b, i, 0)),
        pl.BlockSpec((1, tm, D), lambda b, i: (b, i, 0)),
        pl.BlockSpec((1, N_HEADS, tm, HEAD_SLAB), lambda b, i: (b, 0, i, 0)),
        pl.BlockSpec((1, N_HEADS, tm, HEAD_SLAB), lambda b, i: (b, 0, i, 0)),
        pl.BlockSpec((1, N_HEADS, 1, V_DIM, tm), lambda b, i: (b, 0, i, 0, 0)),
    ]
    return pl.pallas_call(
        _pre_kernel, grid=grid, in_specs=in_specs, out_specs=out_specs, out_shape=out_shape,
        compiler_params=pltpu.CompilerParams(
            dimension_semantics=("parallel", "parallel"), vmem_limit_bytes=V7X_VMEM_LIMIT),
        name="pre",
    )(x, pos3, *consts)


def _attn_kernel(bounded_ref, q_ref, k_ref, vt_ref, o_ref, qt_ref, st_ref, m_ref, l_ref, acc_ref):
    sub = TK // TM_PRE
    nk = k_ref.shape[2] // TK
    qt_ref[...] = q_ref[0, 0].T

    def scores(j, slot):
        start = j * TK if isinstance(j, int) else pl.multiple_of(j * TK, TK)
        kblk = k_ref[0, 0, pl.ds(start, TK), :]
        st_ref[slot] = _dot(kblk, qt_ref[...])

    def values_t(j):
        return jnp.concatenate([vt_ref[0, 0, j * sub + s] for s in range(sub)], axis=1)

    def online_step(j, slot):
        st = st_ref[slot]
        m_old = m_ref[...]
        m_new = jnp.maximum(m_old, jnp.max(st, axis=0, keepdims=True))
        p = jnp.exp2(st - m_new)
        alpha = jnp.exp2(m_old - m_new)
        l_ref[0:1, :] = alpha * l_ref[0:1, :] + jnp.sum(p, axis=0, keepdims=True)
        acc_ref[...] = alpha * acc_ref[...] + _dot(values_t(j), p.astype(_BF16))
        m_ref[...] = m_new

    def run_bounded():
        l = jnp.zeros(l_ref.shape, _F32)
        acc = jnp.zeros(acc_ref.shape, _F32)
        for j in range(nk):
            p = jnp.exp2(_dot(k_ref[0, 0, j * TK:(j + 1) * TK, :], qt_ref[...]))
            l = l + jnp.sum(p.reshape(TK // 8, 8, p.shape[1]), axis=0)
            acc = acc + _dot(values_t(j), p.astype(_BF16))
        l_ref[...] = l
        acc_ref[...] = acc

    def run_online():
        scores(0, 0)

        def pair(jj, carry):
            j = 2 * jj
            scores(j + 1, 1)
            online_step(j, 0)
            scores(j + 2, 0)
            online_step(j + 1, 1)
            return carry

        lax.fori_loop(0, nk // 2 - 1, pair, 0)
        scores(nk - 1, 1)
        online_step(nk - 2, 0)
        online_step(nk - 1, 1)

    m_ref[...] = jnp.full(m_ref.shape, -jnp.inf, _F32)
    l_ref[...] = jnp.zeros(l_ref.shape, _F32)
    acc_ref[...] = jnp.zeros(acc_ref.shape, _F32)
    lax.cond(bounded_ref[0] == 1, run_bounded, run_online)
    l = jnp.sum(l_ref[...], axis=0, keepdims=True)
    o_ref[0] = (acc_ref[...] / l).T.astype(o_ref.dtype)


def _attn_call(bounded, q, k, vt):
    B, H, S, _ = q.shape
    assert S % (2 * TK) == 0 and S % TQ == 0 and TK % TM_PRE == 0
    grid = (B, H, S // TQ)
    return pl.pallas_call(
        _attn_kernel, grid=grid,
        in_specs=[
            pl.BlockSpec(memory_space=pltpu.SMEM),
            pl.BlockSpec((1, 1, TQ, HEAD_SLAB), lambda b, h, i: (b, h, i, 0)),
            pl.BlockSpec((1, 1, S, HEAD_SLAB), lambda b, h, i: (b, h, 0, 0)),
            pl.BlockSpec((1, 1, S // TM_PRE, V_DIM, TM_PRE), lambda b, h, i: (b, h, 0, 0, 0)),
        ],
        out_specs=pl.BlockSpec((1, TQ, V_DIM), lambda b, h, i: (b, i, h)),
        out_shape=jax.ShapeDtypeStruct((B, S, H * V_DIM), _BF16),
        scratch_shapes=[
            pltpu.VMEM((HEAD_SLAB, TQ), _BF16),
            pltpu.VMEM((2, TK, TQ), _F32),
            pltpu.VMEM((1, TQ), _F32),
            pltpu.VMEM((8, TQ), _F32),
            pltpu.VMEM((V_DIM, TQ), _F32),
        ],
        compiler_params=pltpu.CompilerParams(
            dimension_semantics=("parallel", "parallel", "arbitrary"), vmem_limit_bytes=V7X_VMEM_LIMIT),
        name="attn",
    )(bounded, q, k, vt)


def _merge_kernel(x_ref, o_ref, gya_ref, gb_ref, wbo_ref, wout_ref, g2_ref, h_ref, hn_ref):
    yb = _dot(o_ref[0], wbo_ref[...])
    merged = (gya_ref[0].astype(_F32) + gb_ref[0].astype(_F32) * yb).astype(_BF16)
    h = x_ref[0] + _dot(merged, wout_ref[...])
    h_ref[0] = h
    hn_ref[0] = _rms(h, g2_ref[...]).astype(hn_ref.dtype)


def _merge_call(x, o, gya, gb, wbo, wout, g2):
    B, S, D = x.shape
    tm = TM_MERGE
    row = lambda b, i: (b, i, 0)
    return pl.pallas_call(
        _merge_kernel, grid=(B, S // tm),
        in_specs=[pl.BlockSpec((1, tm, D), row)] * 4 + [_const_spec(wbo.shape), _const_spec(wout.shape),
                                                        _const_spec(g2.shape)],
        out_specs=[pl.BlockSpec((1, tm, D), row)] * 2,
        out_shape=[jax.ShapeDtypeStruct((B, S, D), _F32), jax.ShapeDtypeStruct((B, S, D), _BF16)],
        compiler_params=pltpu.CompilerParams(
            dimension_semantics=("parallel", "parallel"), vmem_limit_bytes=V7X_VMEM_LIMIT),
        name="merge",
    )(x, o, gya, gb, wbo, wout, g2)


def _ffn_kernel(h_ref, hn_ref, prev_ref, next_ref, wup_ref, cw_ref, cb_ref, wdown_ref, out_ref,
                ext_ref, up_ref, act_ref):
    i = pl.program_id(1)
    nt = pl.num_programs(1)
    tm = hn_ref.shape[1]
    ext_ref[0:HALO] = prev_ref[0]
    ext_ref[HALO:HALO + tm] = hn_ref[0]
    ext_ref[HALO + tm:] = next_ref[0]
    keep_prev = (i > 0).astype(_F32)
    keep_next = (i < nt - 1).astype(_F32)

    def up_proj(c, slot):
        for part, col in enumerate((c * FF_CHUNK, D_FF + c * FF_CHUNK)):
            up = _dot(ext_ref[...], wup_ref[:, col:col + FF_CHUNK])
            up_ref[slot, part, 0:HALO] = up[:HALO] * keep_prev
            up_ref[slot, part, HALO:HALO + tm] = up[HALO:HALO + tm]
            up_ref[slot, part, HALO + tm:] = up[HALO + tm:] * keep_next

    def conv(slot, part, col):
        cols = slice(col, col + FF_CHUNK)
        return (up_ref[slot, part, HALO - 1:HALO - 1 + tm] * cw_ref[0:1, cols]
                + up_ref[slot, part, HALO:HALO + tm] * cw_ref[1:2, cols]
                + up_ref[slot, part, HALO + 1:HALO + 1 + tm] * cw_ref[2:3, cols] + cb_ref[:, cols])

    n_chunks = D_FF // FF_CHUNK
    up_proj(0, 0)
    for c in range(n_chunks):
        slot = c % 2
        if c + 1 < n_chunks:
            up_proj(c + 1, 1 - slot)
        act_ref[:, c * FF_CHUNK:(c + 1) * FF_CHUNK] = (
            jax.nn.silu(conv(slot, 1, D_FF + c * FF_CHUNK)) * conv(slot, 0, c * FF_CHUNK)).astype(_BF16)
    out_ref[0] = h_ref[0] + _dot(act_ref[...], wdown_ref[...])


def _ffn_call(h, hn, wup, cw, cb, wdown):
    B, S, D = h.shape
    tm = TM_FFN
    nt = S // tm
    per = tm // HALO
    row = lambda b, i: (b, i, 0)
    return pl.pallas_call(
        _ffn_kernel, grid=(B, nt),
        in_specs=[
            pl.BlockSpec((1, tm, D), row),
            pl.BlockSpec((1, tm, D), row),
            pl.BlockSpec((1, HALO, D), lambda b, i: (b, jnp.maximum(i * per - 1, 0), 0)),
            pl.BlockSpec((1, HALO, D), lambda b, i: (b, jnp.minimum((i + 1) * per, S // HALO - 1), 0)),
            _const_spec(wup.shape), _const_spec(cw.shape), _const_spec(cb.shape), _const_spec(wdown.shape),
        ],
        out_specs=pl.BlockSpec((1, tm, D), row),
        out_shape=jax.ShapeDtypeStruct((B, S, D), _F32),
        scratch_shapes=[
            pltpu.VMEM((tm + 2 * HALO, D), _BF16),
            pltpu.VMEM((2, 2, tm + 2 * HALO, FF_CHUNK), _F32),
            pltpu.VMEM((tm, D_FF), _BF16),
        ],
        compiler_params=pltpu.CompilerParams(
            dimension_semantics=("parallel", "parallel"), vmem_limit_bytes=V7X_VMEM_LIMIT),
        name="ffn",
    )(h, hn, hn, hn, wup, cw, cb, wdown)


def _pad_head_cols(w):
    lead = w.shape[:-1]
    w = w.reshape(lead + (N_HEADS, QK_DIM))
    w = jnp.pad(w, [(0, 0)] * len(lead) + [(0, 0), (0, HEAD_SLAB - QK_DIM)])
    return w.reshape(lead + (N_HEADS * HEAD_SLAB,))


def kernel(x, positions, norm1_g, w_in, v_ln_g, v_ln_b, w_s, b_s, w_a_o, q_norm_g, w_uq, kv_norm_g, w_ukv,
           q_head_g, k_head_g, w_b_o, w_out, norm2_g, w_up, conv_w, conv_b, w_down):
    B, S, D = x.shape
    bf = lambda a: a.astype(_BF16)
    row = lambda a: a.reshape(1, -1).astype(_F32)

    c0, c1, c2, c3, c4 = D_A, 2 * D_A, 2 * D_A + Q_RANK, 2 * D_A + Q_RANK + KV_RANK, 2 * D_A + Q_RANK + KV_RANK + QK_ROPE
    w_kr = jnp.pad(w_in[:, c3:c4], ((0, 0), (0, LANES - QK_ROPE)))
    pad_g = lambda g: jnp.pad(g, (0, HEAD_SLAB - QK_DIM)).reshape(1, HEAD_SLAB).astype(_F32)
    inv_freq = ROPE_THETA ** (-jnp.arange(0, QK_ROPE, 2, dtype=_F32) / QK_ROPE)
    zeros = jnp.zeros((LANES - QK_ROPE,), _F32)
    freq = jnp.tile(inv_freq, ROPE_PACK).reshape(1, LANES)
    sign = jnp.concatenate([-jnp.ones((ROPE_HALF,), _F32), jnp.ones((ROPE_HALF,), _F32), zeros]).reshape(1, LANES)

    bound = (QSCALE_LOG2 * QK_DIM * (1.0 + 2.0 ** -6)) * jnp.max(jnp.abs(q_head_g)) * jnp.max(jnp.abs(k_head_g))
    bounded = bound <= SCORE_BOUND_MAX
    bias = jnp.zeros((2, LANES), _F32).at[0, BIAS_LANE].set(1.0).at[1, BIAS_LANE].set(
        jnp.where(bounded, -bound, 0.0).astype(_F32))

    consts = [
        row(norm1_g), bf(w_in[:, :c0]), bf(w_in[:, c0:c1]), bf(w_in[:, c1:c2]), bf(w_in[:, c2:c3]), bf(w_kr),
        bf(w_in[:, c4:c4 + D]), bf(w_in[:, c4 + D:]),
        row(v_ln_g), row(v_ln_b), bf(w_s), jnp.transpose(b_s).astype(_F32), bf(w_a_o),
        row(q_norm_g), bf(_pad_head_cols(w_uq)), row(kv_norm_g), bf(w_ukv),
        pad_g(q_head_g), pad_g(k_head_g), freq, sign, bias,
    ]
    gya, gb, q, k, vt = _pre_call(x, positions.reshape(B, 1, S), consts)
    o = _attn_call(bounded.astype(jnp.int32).reshape(1), q, k, vt)
    h, hn = _merge_call(x, o, gya, gb, bf(w_b_o), bf(w_out), row(norm2_g))
    return _ffn_call(h, hn, bf(w_up), conv_w.astype(_F32), row(conv_b), bf(w_down))
```

```python
import functools
import math

import jax
import jax.numpy as jnp
from jax import lax
from jax.experimental import pallas as pl
from jax.experimental.pallas import tpu as pltpu

D_MODEL = 1024
CHUNK = 128
A_GROUPS = 8
A_GROUP_DIM = 128
D_A = A_GROUPS * A_GROUP_DIM
N_HEADS = 8
Q_RANK = 384
KV_RANK = 256
QK_NOPE = 128
QK_ROPE = 64
V_DIM = 128
QK_DIM = QK_NOPE + QK_ROPE
ROPE_THETA = 10000.0
D_FF = 2816
EPS = 1e-6

LANES = 128
HEAD_SLAB = 2 * LANES
ROPE_HALF = QK_ROPE // 2
ROPE_PACK = LANES // ROPE_HALF
BIAS_LANE = QK_ROPE
QSCALE_LOG2 = math.log2(math.e) / math.sqrt(QK_DIM)
SCORE_BOUND_MAX = 40.0
V7X_VMEM_LIMIT = 56 * 1024 * 1024

TM_PRE = 512
TQ = 1024
TK = 1024
TM_MERGE = 1024
TM_FFN = 512
FF_CHUNK = 256
HALO = 16

_BF16 = jnp.bfloat16
_F32 = jnp.float32


def _dot(a, b):
    return jnp.dot(a, b, preferred_element_type=_F32)


def _gelu(x):
    c = 2.0 * math.sqrt(2.0 / math.pi) * math.log2(math.e)
    z = x * (x * x * (-0.044715 * c) - c)
    return x / (1.0 + jnp.exp2(z))


def _rms(x, g):
    return x * lax.rsqrt(jnp.mean(x * x, axis=-1, keepdims=True) + EPS) * g


def _rope_partner(t):
    lane = lax.broadcasted_iota(jnp.int32, t.shape, 1)
    return jnp.where(lane < ROPE_HALF, pltpu.roll(t, LANES - ROPE_HALF, 1), pltpu.roll(t, ROPE_HALF, 1))


def _const_spec(shape):
    nd = len(shape)
    return pl.BlockSpec(shape, lambda *_: (0,) * nd, pipeline_mode=pl.Buffered(1))


def _pre_kernel(x_ref, pos_ref, g1_ref, wu_ref, wv_ref, wcq_ref, wckv_ref, wkr_ref, wga_ref, wgb_ref,
                lng_ref, lnb_ref, ws_ref, bs_ref, wao_ref, qng_ref, wuq_ref, kvng_ref, wukv_ref,
                qg_ref, kg_ref, freq_ref, sign_ref, bias_ref,
                gya_ref, gb_ref, q_ref, k_ref, vt_ref):
    tm = x_ref.shape[1]
    xn = _rms(x_ref[0], g1_ref[...]).astype(_BF16)

    cq = _dot(xn, wcq_ref[...])
    ckv = _dot(xn, wckv_ref[...])
    kpe = _dot(xn, wkr_ref[...])
    v_raw = _dot(xn, wv_ref[...])
    qall = _dot(_rms(cq, qng_ref[...]).astype(_BF16), wuq_ref[...])
    kv = _dot(_rms(ckv, kvng_ref[...]).astype(_BF16), wukv_ref[...])
    gb_ref[0] = jax.nn.sigmoid(_dot(xn, wgb_ref[...])).astype(gb_ref.dtype)
    u_raw = _dot(xn, wu_ref[...])
    ga_raw = _dot(xn, wga_ref[...])

    gv = _gelu(v_raw)
    mu = jnp.mean(gv, axis=-1, keepdims=True)
    vc = gv - mu
    vln = (vc * lax.rsqrt(jnp.mean(vc * vc, axis=-1, keepdims=True) + EPS) * lng_ref[...]
           + lnb_ref[...]).astype(_BF16)
    rows = []
    for c in range(tm // CHUNK):
        cols = []
        for g in range(A_GROUPS):
            blk = vln[c * CHUNK:(c + 1) * CHUNK, g * A_GROUP_DIM:(g + 1) * A_GROUP_DIM]
            cols.append(_dot(ws_ref[g], blk) + bs_ref[:, g:g + 1])
        rows.append(jnp.concatenate(cols, axis=1))
    mixed = jnp.concatenate(rows, axis=0)
    y = (_gelu(u_raw) * mixed).astype(_BF16)
    gya_ref[0] = (jax.nn.sigmoid(ga_raw) * _dot(y, wao_ref[...])).astype(gya_ref.dtype)

    pos_col = jnp.broadcast_to(pos_ref[0].astype(_F32), (LANES, tm)).T
    rg = tm // ROPE_PACK
    lane = lax.broadcasted_iota(jnp.int32, (rg, LANES), 1)
    pos_packed = pos_col[(ROPE_PACK - 1) * rg:]
    for g in range(ROPE_PACK - 2, -1, -1):
        pos_packed = jnp.where(lane < (g + 1) * ROPE_HALF, pos_col[g * rg:(g + 1) * rg], pos_packed)
    ang = pos_packed * freq_ref[...]
    cos_p, sin_p = jnp.cos(ang), jnp.sin(ang)

    def unpack(tab):
        tiles = []
        shifted = lambda s: tab if s % LANES == 0 else pltpu.roll(tab, s % LANES, 1)
        for g in range(ROPE_PACK):
            lo = shifted(-g * ROPE_HALF)
            hi = shifted((1 - g) * ROPE_HALF)
            tiles.append(jnp.where(lane < ROPE_HALF, lo, hi))
        return jnp.concatenate(tiles, axis=0)

    cos_t = unpack(cos_p)
    sin_t = unpack(sin_p) * sign_ref[...]

    def rope(t):
        return t * cos_t + _rope_partner(t) * sin_t

    for h in range(N_HEADS):
        qh = qall[:, h * HEAD_SLAB:(h + 1) * HEAD_SLAB]
        r = lax.rsqrt(jnp.sum(qh * qh, axis=-1, keepdims=True) * (1.0 / QK_DIM) + EPS) * QSCALE_LOG2
        qn = qh * r * qg_ref[...]
        q_ref[0, h] = jnp.concatenate([qn[:, :LANES], rope(qn[:, LANES:]) + bias_ref[0:1, :]],
                                      axis=1).astype(q_ref.dtype)

    kpe_ss = jnp.sum(kpe * kpe, axis=-1, keepdims=True)
    kpe_rot = rope(kpe * kg_ref[:, LANES:])
    for h in range(N_HEADS):
        kn = kv[:, h * HEAD_SLAB:h * HEAD_SLAB + LANES]
        r = lax.rsqrt((jnp.sum(kn * kn, axis=-1, keepdims=True) + kpe_ss) * (1.0 / QK_DIM) + EPS)
        k_ref[0, h] = jnp.concatenate([kn * r * kg_ref[:, :LANES], kpe_rot * r + bias_ref[1:2, :]],
                                      axis=1).astype(k_ref.dtype)
        vt_ref[0, h, 0] = kv[:, h * HEAD_SLAB + LANES:(h + 1) * HEAD_SLAB].T.astype(vt_ref.dtype)


def _pre_call(x, pos3, consts):
    B, S, D = x.shape
    tm = TM_PRE
    nt = S // tm
    grid = (B, nt)
    in_specs = [
        pl.BlockSpec((1, tm, D), lambda b, i: (b, i, 0)),
        pl.BlockSpec((1, 1, tm), lambda b, i: (b, 0, i)),
    ] + [_const_spec(c.shape) for c in consts]
    out_shape = [
        jax.ShapeDtypeStruct((B, S, D), _BF16),
        jax.ShapeDtypeStruct((B, S, D), _BF16),
        jax.ShapeDtypeStruct((B, N_HEADS, S, HEAD_SLAB), _BF16),
        jax.ShapeDtypeStruct((B, N_HEADS, S, HEAD_SLAB), _BF16),
        jax.ShapeDtypeStruct((B, N_HEADS, nt, V_DIM, tm), _BF16),
    ]
    out_specs = [
        pl.BlockSpec((1, tm, D), lambda b, i: (b, i, 0)),
        pl.BlockSpec((1, tm, D), lambda b, i: (b, i, 0)),
        pl.BlockSpec((1, N_HEADS, tm, HEAD_SLAB), lambda b, i: (b, 0, i, 0)),
        pl.BlockSpec((1, N_HEADS, tm, HEAD_SLAB), lambda b, i: (b, 0, i, 0)),
        pl.BlockSpec((1, N_HEADS, 1, V_DIM, tm), lambda b, i: (b, 0, i, 0, 0)),
    ]
    return pl.pallas_call(
        _pre_kernel, grid=grid, in_specs=in_specs, out_specs=out_specs, out_shape=out_shape,
        compiler_params=pltpu.CompilerParams(
            dimension_semantics=("parallel", "parallel"), vmem_limit_bytes=V7X_VMEM_LIMIT),
        name="pre",
    )(x, pos3, *consts)


def _attn_kernel(bounded_ref, q_ref, k_ref, vt_ref, o_ref, qt_ref, st_ref, m_ref, l_ref, acc_ref):
    sub = TK // TM_PRE
    nk = k_ref.shape[2] // TK
    qt_ref[...] = q_ref[0, 0].T

    def scores(j, slot):
        start = j * TK if isinstance(j, int) else pl.multiple_of(j * TK, TK)
        kblk = k_ref[0, 0, pl.ds(start, TK), :]
        st_ref[slot] = _dot(kblk, qt_ref[...])

    def values_t(j):
        return jnp.concatenate([vt_ref[0, 0, j * sub + s] for s in range(sub)], axis=1)

    def online_step(j, slot):
        st = st_ref[slot]
        m_old = m_ref[...]
        m_new = jnp.maximum(m_old, jnp.max(st, axis=0, keepdims=True))
        p = jnp.exp2(st - m_new)
        alpha = jnp.exp2(m_old - m_new)
        l_ref[0:1, :] = alpha * l_ref[0:1, :] + jnp.sum(p, axis=0, keepdims=True)
        acc_ref[...] = alpha * acc_ref[...] + _dot(values_t(j), p.astype(_BF16))
        m_ref[...] = m_new

    def run_bounded():
        l = jnp.zeros(l_ref.shape, _F32)
        acc = jnp.zeros(acc_ref.shape, _F32)
        for j in range(nk):
            p = jnp.exp2(_dot(k_ref[0, 0, j * TK:(j + 1) * TK, :], qt_ref[...]))
            l = l + jnp.sum(p.reshape(TK // 8, 8, p.shape[1]), axis=0)
            acc = acc + _dot(values_t(j), p.astype(_BF16))
        l_ref[...] = l
        acc_ref[...] = acc

    def run_online():
        scores(0, 0)

        def pair(jj, carry):
            j = 2 * jj
            scores(j + 1, 1)
            online_step(j, 0)
            scores(j + 2, 0)
            online_step(j + 1, 1)
            return carry

        lax.fori_loop(0, nk // 2 - 1, pair, 0)
        scores(nk - 1, 1)
        online_step(nk - 2, 0)
        online_step(nk - 1, 1)

    m_ref[...] = jnp.full(m_ref.shape, -jnp.inf, _F32)
    l_ref[...] = jnp.zeros(l_ref.shape, _F32)
    acc_ref[...] = jnp.zeros(acc_ref.shape, _F32)
    lax.cond(bounded_ref[0] == 1, run_bounded, run_online)
    l = jnp.sum(l_ref[...], axis=0, keepdims=True)
    o_ref[0] = (acc_ref[...] / l).T.astype(o_ref.dtype)


def _attn_call(bounded, q, k, vt):
    B, H, S, _ = q.shape
    assert S % (2 * TK) == 0 and S % TQ == 0 and TK % TM_PRE == 0
    grid = (B, H, S // TQ)
    return pl.pallas_call(
        _attn_kernel, grid=grid,
        in_specs=[
            pl.BlockSpec(memory_space=pltpu.SMEM),
            pl.BlockSpec((1, 1, TQ, HEAD_SLAB), lambda b, h, i: (b, h, i, 0)),
            pl.BlockSpec((1, 1, S, HEAD_SLAB), lambda b, h, i: (b, h, 0, 0)),
            pl.BlockSpec((1, 1, S // TM_PRE, V_DIM, TM_PRE), lambda b, h, i: (b, h, 0, 0, 0)),
        ],
        out_specs=pl.BlockSpec((1, TQ, V_DIM), lambda b, h, i: (b, i, h)),
        out_shape=jax.ShapeDtypeStruct((B, S, H * V_DIM), _BF16),
        scratch_shapes=[
            pltpu.VMEM((HEAD_SLAB, TQ), _BF16),
            pltpu.VMEM((2, TK, TQ), _F32),
            pltpu.VMEM((1, TQ), _F32),
            pltpu.VMEM((8, TQ), _F32),
            pltpu.VMEM((V_DIM, TQ), _F32),
        ],
        compiler_params=pltpu.CompilerParams(
            dimension_semantics=("parallel", "parallel", "arbitrary"), vmem_limit_bytes=V7X_VMEM_LIMIT),
        name="attn",
    )(bounded, q, k, vt)


def _merge_kernel(x_ref, o_ref, gya_ref, gb_ref, wbo_ref, wout_ref, g2_ref, h_ref, hn_ref):
    yb = _dot(o_ref[0], wbo_ref[...])
    merged = (gya_ref[0].astype(_F32) + gb_ref[0].astype(_F32) * yb).astype(_BF16)
    h = x_ref[0] + _dot(merged, wout_ref[...])
    h_ref[0] = h
    hn_ref[0] = _rms(h, g2_ref[...]).astype(hn_ref.dtype)


def _merge_call(x, o, gya, gb, wbo, wout, g2):
    B, S, D = x.shape
    tm = TM_MERGE
    row = lambda b, i: (b, i, 0)
    return pl.pallas_call(
        _merge_kernel, grid=(B, S // tm),
        in_specs=[pl.BlockSpec((1, tm, D), row)] * 4 + [_const_spec(wbo.shape), _const_spec(wout.shape),
                                                        _const_spec(g2.shape)],
        out_specs=[pl.BlockSpec((1, tm, D), row)] * 2,
        out_shape=[jax.ShapeDtypeStruct((B, S, D), _F32), jax.ShapeDtypeStruct((B, S, D), _BF16)],
        compiler_params=pltpu.CompilerParams(
            dimension_semantics=("parallel", "parallel"), vmem_limit_bytes=V7X_VMEM_LIMIT),
        name="merge",
    )(x, o, gya, gb, wbo, wout, g2)


def _ffn_kernel(h_ref, hn_ref, prev_ref, next_ref, wup_ref, cw_ref, cb_ref, wdown_ref, out_ref,
                ext_ref, act_ref):
    i = pl.program_id(1)
    nt = pl.num_programs(1)
    tm = hn_ref.shape[1]
    ext_ref[0:HALO] = prev_ref[0]
    ext_ref[HALO:HALO + tm] = hn_ref[0]
    ext_ref[HALO + tm:] = next_ref[0]
    keep_prev = (i > 0).astype(_F32)
    keep_next = (i < nt - 1).astype(_F32)

    def up_proj(c):
        ups = []
        for col in (c * FF_CHUNK, D_FF + c * FF_CHUNK):
            up = _dot(ext_ref[...], wup_ref[:, col:col + FF_CHUNK])
            ups.append(jnp.concatenate(
                [up[:HALO] * keep_prev, up[HALO:HALO + tm], up[HALO + tm:] * keep_next], axis=0))
        return ups

    def conv(up, col):
        cols = slice(col, col + FF_CHUNK)
        rows = up.shape[0]
        return (pltpu.roll(up, 1, 0)[HALO:HALO + tm] * cw_ref[0:1, cols]
                + up[HALO:HALO + tm] * cw_ref[1:2, cols]
                + pltpu.roll(up, rows - 1, 0)[HALO:HALO + tm] * cw_ref[2:3, cols] + cb_ref[:, cols])

    n_chunks = D_FF // FF_CHUNK
    up_next = up_proj(0)
    for c in range(n_chunks):
        up_val, up_gate = up_next
        if c + 1 < n_chunks:
            up_next = up_proj(c + 1)
        act_ref[:, c * FF_CHUNK:(c + 1) * FF_CHUNK] = (
            jax.nn.silu(conv(up_gate, D_FF + c * FF_CHUNK)) * conv(up_val, c * FF_CHUNK)).astype(_BF16)
    out_ref[0] = h_ref[0] + _dot(act_ref[...], wdown_ref[...])


def _ffn_call(h, hn, wup, cw, cb, wdown):
    B, S, D = h.shape
    tm = TM_FFN
    nt = S // tm
    per = tm // HALO
    row = lambda b, i: (b, i, 0)
    return pl.pallas_call(
        _ffn_kernel, grid=(B, nt),
        in_specs=[
            pl.BlockSpec((1, tm, D), row),
            pl.BlockSpec((1, tm, D), row),
            pl.BlockSpec((1, HALO, D), lambda b, i: (b, jnp.maximum(i * per - 1, 0), 0)),
            pl.BlockSpec((1, HALO, D), lambda b, i: (b, jnp.minimum((i + 1) * per, S // HALO - 1), 0)),
            _const_spec(wup.shape), _const_spec(cw.shape), _const_spec(cb.shape), _const_spec(wdown.shape),
        ],
        out_specs=pl.BlockSpec((1, tm, D), row),
        out_shape=jax.ShapeDtypeStruct((B, S, D), _F32),
        scratch_shapes=[
            pltpu.VMEM((tm + 2 * HALO, D), _BF16),
            pltpu.VMEM((tm, D_FF), _BF16),
        ],
        compiler_params=pltpu.CompilerParams(
            dimension_semantics=("parallel", "parallel"), vmem_limit_bytes=V7X_VMEM_LIMIT),
        name="ffn",
    )(h, hn, hn, hn, wup, cw, cb, wdown)


def _pad_head_cols(w):
    lead = w.shape[:-1]
    w = w.reshape(lead + (N_HEADS, QK_DIM))
    w = jnp.pad(w, [(0, 0)] * len(lead) + [(0, 0), (0, HEAD_SLAB - QK_DIM)])
    return w.reshape(lead + (N_HEADS * HEAD_SLAB,))


def kernel(x, positions, norm1_g, w_in, v_ln_g, v_ln_b, w_s, b_s, w_a_o, q_norm_g, w_uq, kv_norm_g, w_ukv,
           q_head_g, k_head_g, w_b_o, w_out, norm2_g, w_up, conv_w, conv_b, w_down):
    B, S, D = x.shape
    bf = lambda a: a.astype(_BF16)
    row = lambda a: a.reshape(1, -1).astype(_F32)

    c0, c1, c2, c3, c4 = D_A, 2 * D_A, 2 * D_A + Q_RANK, 2 * D_A + Q_RANK + KV_RANK, 2 * D_A + Q_RANK + KV_RANK + QK_ROPE
    w_kr = jnp.pad(w_in[:, c3:c4], ((0, 0), (0, LANES - QK_ROPE)))
    pad_g = lambda g: jnp.pad(g, (0, HEAD_SLAB - QK_DIM)).reshape(1, HEAD_SLAB).astype(_F32)
    inv_freq = ROPE_THETA ** (-jnp.arange(0, QK_ROPE, 2, dtype=_F32) / QK_ROPE)
    zeros = jnp.zeros((LANES - QK_ROPE,), _F32)
    freq = jnp.tile(inv_freq, ROPE_PACK).reshape(1, LANES)
    sign = jnp.concatenate([-jnp.ones((ROPE_HALF,), _F32), jnp.ones((ROPE_HALF,), _F32), zeros]).reshape(1, LANES)

    bound = (QSCALE_LOG2 * QK_DIM * (1.0 + 2.0 ** -6)) * jnp.max(jnp.abs(q_head_g)) * jnp.max(jnp.abs(k_head_g))
    bounded = bound <= SCORE_BOUND_MAX
    bias = jnp.zeros((2, LANES), _F32).at[0, BIAS_LANE].set(1.0).at[1, BIAS_LANE].set(
        jnp.where(bounded, -bound, 0.0).astype(_F32))

    consts = [
        row(norm1_g), bf(w_in[:, :c0]), bf(w_in[:, c0:c1]), bf(w_in[:, c1:c2]), bf(w_in[:, c2:c3]), bf(w_kr),
        bf(w_in[:, c4:c4 + D]), bf(w_in[:, c4 + D:]),
        row(v_ln_g), row(v_ln_b), bf(w_s), jnp.transpose(b_s).astype(_F32), bf(w_a_o),
        row(q_norm_g), bf(_pad_head_cols(w_uq)), row(kv_norm_g), bf(w_ukv),
        pad_g(q_head_g), pad_g(k_head_g), freq, sign, bias,
    ]
    gya, gb, q, k, vt = _pre_call(x, positions.reshape(B, 1, S), consts)
    o = _attn_call(bounded.astype(jnp.int32).reshape(1), q, k, vt)
    h, hn = _merge_call(x, o, gya, gb, bf(w_b_o), bf(w_out), row(norm2_g))
    return _ffn_call(h, hn, bf(w_up), conv_w.astype(_F32), row(conv_b), bf(w_down))
```

```python
import math

import jax
import jax.numpy as jnp
from jax import lax
from jax.experimental import pallas as pl
from jax.experimental.pallas import tpu as pltpu

D_MODEL = 1024
CHUNK = 128
A_GROUPS = 8
A_GROUP_DIM = 128
D_A = A_GROUPS * A_GROUP_DIM
N_HEADS = 8
Q_RANK = 384
KV_RANK = 256
QK_NOPE = 128
QK_ROPE = 64
V_DIM = 128
QK_DIM = QK_NOPE + QK_ROPE
ROPE_THETA = 10000.0
D_FF = 2816
EPS = 1e-6

LANES = 128
HEAD_SLAB = 2 * LANES
ROPE_HALF = QK_ROPE // 2
ROPE_PACK = LANES // ROPE_HALF
BIAS_LANE = QK_ROPE
QSCALE_LOG2 = math.log2(math.e) / math.sqrt(QK_DIM)
SCORE_BOUND_MAX = 40.0
V7X_VMEM_LIMIT = 56 * 1024 * 1024

TM_PRE = 512
TQ = 2048
TK = 1024
TM_FFN = 512
FF_CHUNK = 256
HALO = 16

_BF16 = jnp.bfloat16
_F32 = jnp.float32


def _dot(a, b):
    return jnp.dot(a, b, preferred_element_type=_F32)


def _gelu(x):
    c = 2.0 * math.sqrt(2.0 / math.pi) * math.log2(math.e)
    z = x * (x * x * (-0.044715 * c) - c)
    return x / (1.0 + jnp.exp2(z))


def _rms(x, g):
    return x * lax.rsqrt(jnp.mean(x * x, axis=-1, keepdims=True) + EPS) * g


def _rope_partner(t):
    lane = lax.broadcasted_iota(jnp.int32, t.shape, 1)
    return jnp.where(lane < ROPE_HALF, pltpu.roll(t, LANES - ROPE_HALF, 1), pltpu.roll(t, ROPE_HALF, 1))


def _const_spec(shape):
    nd = len(shape)
    return pl.BlockSpec(shape, lambda *_: (0,) * nd, pipeline_mode=pl.Buffered(1))


def _pre_kernel(x_ref, pos_ref, g1_ref, wu_ref, wv_ref, wcq_ref, wckv_ref, wkr_ref, wga_ref, wgb_ref,
                lng_ref, lnb_ref, ws_ref, bs_ref, wao_ref, qng_ref, wuq_ref, kvng_ref, wukv_ref,
                qg_ref, kg_ref, freq_ref, sign_ref, bias_ref,
                gya_ref, gb_ref, q_ref, k_ref, vt_ref):
    tm = x_ref.shape[1]
    xn = _rms(x_ref[0], g1_ref[...]).astype(_BF16)

    cq = _dot(xn, wcq_ref[...])
    ckv = _dot(xn, wckv_ref[...])
    kpe = _dot(xn, wkr_ref[...])
    v_raw = _dot(xn, wv_ref[...])
    qall = _dot(_rms(cq, qng_ref[...]).astype(_BF16), wuq_ref[...])
    kv = _dot(_rms(ckv, kvng_ref[...]).astype(_BF16), wukv_ref[...])
    gb_ref[0] = jax.nn.sigmoid(_dot(xn, wgb_ref[...])).astype(gb_ref.dtype)
    u_raw = _dot(xn, wu_ref[...])
    ga_raw = _dot(xn, wga_ref[...])

    gv = _gelu(v_raw)
    mu = jnp.mean(gv, axis=-1, keepdims=True)
    vc = gv - mu
    vln = (vc * lax.rsqrt(jnp.mean(vc * vc, axis=-1, keepdims=True) + EPS) * lng_ref[...]
           + lnb_ref[...]).astype(_BF16)
    rows = []
    for c in range(tm // CHUNK):
        cols = []
        for g in range(A_GROUPS):
            blk = vln[c * CHUNK:(c + 1) * CHUNK, g * A_GROUP_DIM:(g + 1) * A_GROUP_DIM]
            cols.append(_dot(ws_ref[g], blk) + bs_ref[:, g:g + 1])
        rows.append(jnp.concatenate(cols, axis=1))
    mixed = jnp.concatenate(rows, axis=0)
    y = (_gelu(u_raw) * mixed).astype(_BF16)
    gya_ref[0] = (jax.nn.sigmoid(ga_raw) * _dot(y, wao_ref[...])).astype(gya_ref.dtype)

    pos_col = jnp.broadcast_to(pos_ref[0].astype(_F32), (LANES, tm)).T
    rg = tm // ROPE_PACK
    lane = lax.broadcasted_iota(jnp.int32, (rg, LANES), 1)
    pos_packed = pos_col[(ROPE_PACK - 1) * rg:]
    for g in range(ROPE_PACK - 2, -1, -1):
        pos_packed = jnp.where(lane < (g + 1) * ROPE_HALF, pos_col[g * rg:(g + 1) * rg], pos_packed)
    ang = pos_packed * freq_ref[...]
    cos_p, sin_p = jnp.cos(ang), jnp.sin(ang)

    def unpack(tab):
        tiles = []
        shifted = lambda s: tab if s % LANES == 0 else pltpu.roll(tab, s % LANES, 1)
        for g in range(ROPE_PACK):
            lo = shifted(-g * ROPE_HALF)
            hi = shifted((1 - g) * ROPE_HALF)
            tiles.append(jnp.where(lane < ROPE_HALF, lo, hi))
        return jnp.concatenate(tiles, axis=0)

    cos_t = unpack(cos_p)
    sin_t = unpack(sin_p) * sign_ref[...]

    def rope(t):
        return t * cos_t + _rope_partner(t) * sin_t

    for h in range(N_HEADS):
        qh = qall[:, h * HEAD_SLAB:(h + 1) * HEAD_SLAB]
        r = lax.rsqrt(jnp.sum(qh * qh, axis=-1, keepdims=True) * (1.0 / QK_DIM) + EPS) * QSCALE_LOG2
        qn = qh * r * qg_ref[...]
        q_ref[0, h] = jnp.concatenate([qn[:, :LANES], rope(qn[:, LANES:]) + bias_ref[0:1, :]],
                                      axis=1).astype(q_ref.dtype)

    kpe_ss = jnp.sum(kpe * kpe, axis=-1, keepdims=True)
    kpe_rot = rope(kpe * kg_ref[:, LANES:])
    for h in range(N_HEADS):
        kn = kv[:, h * HEAD_SLAB:h * HEAD_SLAB + LANES]
        r = lax.rsqrt((jnp.sum(kn * kn, axis=-1, keepdims=True) + kpe_ss) * (1.0 / QK_DIM) + EPS)
        k_ref[0, h] = jnp.concatenate([kn * r * kg_ref[:, :LANES], kpe_rot * r + bias_ref[1:2, :]],
                                      axis=1).astype(k_ref.dtype)
        vt_ref[0, h, 0] = kv[:, h * HEAD_SLAB + LANES:(h + 1) * HEAD_SLAB].T.astype(vt_ref.dtype)


def _pre_call(x, pos3, consts):
    B, S, D = x.shape
    tm = TM_PRE
    nt = S // tm
    grid = (B, nt)
    in_specs = [
        pl.BlockSpec((1, tm, D), lambda b, i: (b, i, 0)),
        pl.BlockSpec((1, 1, tm), lambda b, i: (b, 0, i)),
    ] + [_const_spec(c.shape) for c in consts]
    out_shape = [
        jax.ShapeDtypeStruct((B, S, D), _BF16),
        jax.ShapeDtypeStruct((B, S, D), _BF16),
        jax.ShapeDtypeStruct((B, N_HEADS, S, HEAD_SLAB), _BF16),
        jax.ShapeDtypeStruct((B, N_HEADS, S, HEAD_SLAB), _BF16),
        jax.ShapeDtypeStruct((B, N_HEADS, nt, V_DIM, tm), _BF16),
    ]
    out_specs = [
        pl.BlockSpec((1, tm, D), lambda b, i: (b, i, 0)),
        pl.BlockSpec((1, tm, D), lambda b, i: (b, i, 0)),
        pl.BlockSpec((1, N_HEADS, tm, HEAD_SLAB), lambda b, i: (b, 0, i, 0)),
        pl.BlockSpec((1, N_HEADS, tm, HEAD_SLAB), lambda b, i: (b, 0, i, 0)),
        pl.BlockSpec((1, N_HEADS, 1, V_DIM, tm), lambda b, i: (b, 0, i, 0, 0)),
    ]
    return pl.pallas_call(
        _pre_kernel, grid=grid, in_specs=in_specs, out_specs=out_specs, out_shape=out_shape,
        compiler_params=pltpu.CompilerParams(
            dimension_semantics=("parallel", "parallel"), vmem_limit_bytes=V7X_VMEM_LIMIT),
        name="pre",
    )(x, pos3, *consts)


def _attn_kernel(bounded_ref, q_ref, k_ref, vt_ref, o_ref, qt_ref, st_ref, m_ref, l_ref, acc_ref):
    sub = TK // TM_PRE
    nk = k_ref.shape[2] // TK
    qt_ref[...] = q_ref[0, 0].T

    def scores(j, slot):
        start = j * TK if isinstance(j, int) else pl.multiple_of(j * TK, TK)
        kblk = k_ref[0, 0, pl.ds(start, TK), :]
        st_ref[slot] = _dot(kblk, qt_ref[...])

    def values_t(j):
        return jnp.concatenate([vt_ref[0, 0, j * sub + s] for s in range(sub)], axis=1)

    def online_step(j, slot):
        st = st_ref[slot]
        m_old = m_ref[...]
        m_new = jnp.maximum(m_old, jnp.max(st, axis=0, keepdims=True))
        p = jnp.exp2(st - m_new)
        alpha = jnp.exp2(m_old - m_new)
        l_ref[0:1, :] = alpha * l_ref[0:1, :] + jnp.sum(p, axis=0, keepdims=True)
        acc_ref[...] = alpha * acc_ref[...] + _dot(values_t(j), p.astype(_BF16))
        m_ref[...] = m_new

    def run_bounded():
        l = jnp.zeros(l_ref.shape, _F32)
        acc = jnp.zeros(acc_ref.shape, _F32)
        for j in range(nk):
            p = jnp.exp2(_dot(k_ref[0, 0, j * TK:(j + 1) * TK, :], qt_ref[...]))
            l = l + jnp.sum(p.reshape(TK // 8, 8, p.shape[1]), axis=0)
            acc = acc + _dot(values_t(j), p.astype(_BF16))
        l_ref[...] = l
        acc_ref[...] = acc

    def run_online():
        scores(0, 0)

        def pair(jj, carry):
            j = 2 * jj
            scores(j + 1, 1)
            online_step(j, 0)
            scores(j + 2, 0)
            online_step(j + 1, 1)
            return carry

        lax.fori_loop(0, nk // 2 - 1, pair, 0)
        scores(nk - 1, 1)
        online_step(nk - 2, 0)
        online_step(nk - 1, 1)

    m_ref[...] = jnp.full(m_ref.shape, -jnp.inf, _F32)
    l_ref[...] = jnp.zeros(l_ref.shape, _F32)
    acc_ref[...] = jnp.zeros(acc_ref.shape, _F32)
    lax.cond(bounded_ref[0] == 1, run_bounded, run_online)
    l = jnp.sum(l_ref[...], axis=0, keepdims=True)
    o_ref[0] = (acc_ref[...] / l).T.astype(o_ref.dtype)


def _attn_call(bounded, q, k, vt):
    B, H, S, _ = q.shape
    assert S % (2 * TK) == 0 and S % TQ == 0 and TK % TM_PRE == 0
    grid = (B, H, S // TQ)
    return pl.pallas_call(
        _attn_kernel, grid=grid,
        in_specs=[
            pl.BlockSpec(memory_space=pltpu.SMEM),
            pl.BlockSpec((1, 1, TQ, HEAD_SLAB), lambda b, h, i: (b, h, i, 0)),
            pl.BlockSpec((1, 1, S, HEAD_SLAB), lambda b, h, i: (b, h, 0, 0)),
            pl.BlockSpec((1, 1, S // TM_PRE, V_DIM, TM_PRE), lambda b, h, i: (b, h, 0, 0, 0)),
        ],
        out_specs=pl.BlockSpec((1, TQ, V_DIM), lambda b, h, i: (b, i, h)),
        out_shape=jax.ShapeDtypeStruct((B, S, H * V_DIM), _BF16),
        scratch_shapes=[
            pltpu.VMEM((HEAD_SLAB, TQ), _BF16),
            pltpu.VMEM((2, TK, TQ), _F32),
            pltpu.VMEM((1, TQ), _F32),
            pltpu.VMEM((8, TQ), _F32),
            pltpu.VMEM((V_DIM, TQ), _F32),
        ],
        compiler_params=pltpu.CompilerParams(
            dimension_semantics=("parallel", "parallel", "arbitrary"), vmem_limit_bytes=V7X_VMEM_LIMIT),
        name="attn",
    )(bounded, q, k, vt)


def _mix_ffn_kernel(x_ref, xp_ref, xn_ref, o_ref, op_ref, on_ref, gya_ref, gyap_ref, gyan_ref,
                    gb_ref, gbp_ref, gbn_ref, wbo_ref, wout_ref, g2_ref, wup_ref, cw_ref, cb_ref, wdown_ref,
                    out_ref, ext_ref, act_ref):
    i = pl.program_id(1)
    nt = pl.num_programs(1)
    tm = x_ref.shape[1]

    def rows3(prev, main, nxt):
        return jnp.concatenate([prev[0], main[0], nxt[0]], axis=0)

    yb = _dot(rows3(op_ref, o_ref, on_ref), wbo_ref[...])
    merged = (rows3(gyap_ref, gya_ref, gyan_ref).astype(_F32)
              + rows3(gbp_ref, gb_ref, gbn_ref).astype(_F32) * yb).astype(_BF16)
    h_ext = rows3(xp_ref, x_ref, xn_ref) + _dot(merged, wout_ref[...])
    ext_ref[...] = _rms(h_ext, g2_ref[...]).astype(_BF16)
    out_ref[0] = h_ext[HALO:HALO + tm]

    keep_prev = (i > 0).astype(_F32)
    keep_next = (i < nt - 1).astype(_F32)

    def up_proj(c):
        ups = []
        for col in (c * FF_CHUNK, D_FF + c * FF_CHUNK):
            up = _dot(ext_ref[...], wup_ref[:, col:col + FF_CHUNK])
            ups.append(jnp.concatenate(
                [up[:HALO] * keep_prev, up[HALO:HALO + tm], up[HALO + tm:] * keep_next], axis=0))
        return ups

    def conv(up, col):
        cols = slice(col, col + FF_CHUNK)
        rows = up.shape[0]
        return (pltpu.roll(up, 1, 0)[HALO:HALO + tm] * cw_ref[0:1, cols]
                + up[HALO:HALO + tm] * cw_ref[1:2, cols]
                + pltpu.roll(up, rows - 1, 0)[HALO:HALO + tm] * cw_ref[2:3, cols] + cb_ref[:, cols])

    n_chunks = D_FF // FF_CHUNK
    up_next = up_proj(0)
    for c in range(n_chunks):
        up_val, up_gate = up_next
        if c + 1 < n_chunks:
            up_next = up_proj(c + 1)
        act_ref[:, c * FF_CHUNK:(c + 1) * FF_CHUNK] = (
            jax.nn.silu(conv(up_gate, D_FF + c * FF_CHUNK)) * conv(up_val, c * FF_CHUNK)).astype(_BF16)
    out_ref[0] += _dot(act_ref[...], wdown_ref[...])


def _mix_ffn_call(x, o, gya, gb, wbo, wout, g2, wup, cw, cb, wdown):
    B, S, D = x.shape
    tm = TM_FFN
    nt = S // tm
    per = tm // HALO
    main = pl.BlockSpec((1, tm, D), lambda b, i: (b, i, 0))
    prev = pl.BlockSpec((1, HALO, D), lambda b, i: (b, jnp.maximum(i * per - 1, 0), 0))
    nxt = pl.BlockSpec((1, HALO, D), lambda b, i: (b, jnp.minimum((i + 1) * per, S // HALO - 1), 0))
    consts = (wbo, wout, g2, wup, cw, cb, wdown)
    return pl.pallas_call(
        _mix_ffn_kernel, grid=(B, nt),
        in_specs=[main, prev, nxt] * 4 + [_const_spec(c.shape) for c in consts],
        out_specs=main,
        out_shape=jax.ShapeDtypeStruct((B, S, D), _F32),
        scratch_shapes=[
            pltpu.VMEM((tm + 2 * HALO, D), _BF16),
            pltpu.VMEM((tm, D_FF), _BF16),
        ],
        compiler_params=pltpu.CompilerParams(
            dimension_semantics=("parallel", "parallel"), vmem_limit_bytes=V7X_VMEM_LIMIT),
        name="mix_ffn",
    )(x, x, x, o, o, o, gya, gya, gya, gb, gb, gb, *consts)


def _pad_head_cols(w):
    lead = w.shape[:-1]
    w = w.reshape(lead + (N_HEADS, QK_DIM))
    w = jnp.pad(w, [(0, 0)] * len(lead) + [(0, 0), (0, HEAD_SLAB - QK_DIM)])
    return w.reshape(lead + (N_HEADS * HEAD_SLAB,))


def kernel(x, positions, norm1_g, w_in, v_ln_g, v_ln_b, w_s, b_s, w_a_o, q_norm_g, w_uq, kv_norm_g, w_ukv,
           q_head_g, k_head_g, w_b_o, w_out, norm2_g, w_up, conv_w, conv_b, w_down):
    B, S, D = x.shape
    bf = lambda a: a.astype(_BF16)
    row = lambda a: a.reshape(1, -1).astype(_F32)

    c0, c1, c2, c3, c4 = D_A, 2 * D_A, 2 * D_A + Q_RANK, 2 * D_A + Q_RANK + KV_RANK, 2 * D_A + Q_RANK + KV_RANK + QK_ROPE
    w_kr = jnp.pad(w_in[:, c3:c4], ((0, 0), (0, LANES - QK_ROPE)))
    pad_g = lambda g: jnp.pad(g, (0, HEAD_SLAB - QK_DIM)).reshape(1, HEAD_SLAB).astype(_F32)
    inv_freq = ROPE_THETA ** (-jnp.arange(0, QK_ROPE, 2, dtype=_F32) / QK_ROPE)
    zeros = jnp.zeros((LANES - QK_ROPE,), _F32)
    freq = jnp.tile(inv_freq, ROPE_PACK).reshape(1, LANES)
    sign = jnp.concatenate([-jnp.ones((ROPE_HALF,), _F32), jnp.ones((ROPE_HALF,), _F32), zeros]).reshape(1, LANES)

    bound = (QSCALE_LOG2 * QK_DIM * (1.0 + 2.0 ** -6)) * jnp.max(jnp.abs(q_head_g)) * jnp.max(jnp.abs(k_head_g))
    bounded = bound <= SCORE_BOUND_MAX
    bias = jnp.zeros((2, LANES), _F32).at[0, BIAS_LANE].set(1.0).at[1, BIAS_LANE].set(
        jnp.where(bounded, -bound, 0.0).astype(_F32))

    consts = [
        row(norm1_g), bf(w_in[:, :c0]), bf(w_in[:, c0:c1]), bf(w_in[:, c1:c2]), bf(w_in[:, c2:c3]), bf(w_kr),
        bf(w_in[:, c4:c4 + D]), bf(w_in[:, c4 + D:]),
        row(v_ln_g), row(v_ln_b), bf(w_s), jnp.transpose(b_s).astype(_F32), bf(w_a_o),
        row(q_norm_g), bf(_pad_head_cols(w_uq)), row(kv_norm_g), bf(w_ukv),
        pad_g(q_head_g), pad_g(k_head_g), freq, sign, bias,
    ]
    gya, gb, q, k, vt = _pre_call(x, positions.reshape(B, 1, S), consts)
    o = _attn_call(bounded.astype(jnp.int32).reshape(1), q, k, vt)
    return _mix_ffn_call(x, o, gya, gb, bf(w_b_o), bf(w_out), row(norm2_g),
                         bf(w_up), conv_w.astype(_F32), row(conv_b), bf(w_down))
```

```python
import math

import jax
import jax.numpy as jnp
from jax import lax
from jax.experimental import pallas as pl
from jax.experimental.pallas import tpu as pltpu

D_MODEL = 1024
CHUNK = 128
A_GROUPS = 8
A_GROUP_DIM = 128
D_A = A_GROUPS * A_GROUP_DIM
N_HEADS = 8
Q_RANK = 384
KV_RANK = 256
QK_NOPE = 128
QK_ROPE = 64
V_DIM = 128
QK_DIM = QK_NOPE + QK_ROPE
ROPE_THETA = 10000.0
D_FF = 2816
EPS = 1e-6

LANES = 128
HEAD_SLAB = 2 * LANES
ROPE_HALF = QK_ROPE // 2
ROPE_PACK = LANES // ROPE_HALF
BIAS_LANE = QK_ROPE
QSCALE_LOG2 = math.log2(math.e) / math.sqrt(QK_DIM)
SCORE_BOUND_MAX = 40.0
V7X_VMEM_LIMIT = 56 * 1024 * 1024

TM_PRE = 512
TQ = 2048
TQ_SUB = 1024
TK = 1024
TM_FFN = 512
FF_CHUNK = 256
HALO = 16

_BF16 = jnp.bfloat16
_F32 = jnp.float32


def _dot(a, b):
    return jnp.dot(a, b, preferred_element_type=_F32)


def _gelu(x):
    c = 2.0 * math.sqrt(2.0 / math.pi) * math.log2(math.e)
    z = x * (x * x * (-0.044715 * c) - c)
    return x / (1.0 + jnp.exp2(z))


def _rms(x, g):
    return x * lax.rsqrt(jnp.mean(x * x, axis=-1, keepdims=True) + EPS) * g


def _rope_partner(t):
    lane = lax.broadcasted_iota(jnp.int32, t.shape, 1)
    return jnp.where(lane < ROPE_HALF, pltpu.roll(t, LANES - ROPE_HALF, 1), pltpu.roll(t, ROPE_HALF, 1))


def _const_spec(shape):
    nd = len(shape)
    return pl.BlockSpec(shape, lambda *_: (0,) * nd, pipeline_mode=pl.Buffered(1))


def _pre_kernel(x_ref, pos_ref, g1_ref, wu_ref, wv_ref, wcq_ref, wckv_ref, wkr_ref, wga_ref, wgb_ref,
                lng_ref, lnb_ref, ws_ref, bs_ref, wao_ref, qng_ref, wuq_ref, kvng_ref, wukv_ref,
                qg_ref, kg_ref, freq_ref, sign_ref, bias_ref,
                gya_ref, gb_ref, q_ref, k_ref, vt_ref):
    tm = x_ref.shape[1]
    xn = _rms(x_ref[0], g1_ref[...]).astype(_BF16)

    cq = _dot(xn, wcq_ref[...])
    ckv = _dot(xn, wckv_ref[...])
    kpe = _dot(xn, wkr_ref[...])
    v_raw = _dot(xn, wv_ref[...])
    qall = _dot(_rms(cq, qng_ref[...]).astype(_BF16), wuq_ref[...])
    kv = _dot(_rms(ckv, kvng_ref[...]).astype(_BF16), wukv_ref[...])
    gb_ref[0] = jax.nn.sigmoid(_dot(xn, wgb_ref[...])).astype(gb_ref.dtype)
    u_raw = _dot(xn, wu_ref[...])
    ga_raw = _dot(xn, wga_ref[...])

    gv = _gelu(v_raw)
    mu = jnp.mean(gv, axis=-1, keepdims=True)
    vc = gv - mu
    vln = (vc * lax.rsqrt(jnp.mean(vc * vc, axis=-1, keepdims=True) + EPS) * lng_ref[...]
           + lnb_ref[...]).astype(_BF16)
    rows = []
    for c in range(tm // CHUNK):
        cols = []
        for g in range(A_GROUPS):
            blk = vln[c * CHUNK:(c + 1) * CHUNK, g * A_GROUP_DIM:(g + 1) * A_GROUP_DIM]
            cols.append(_dot(ws_ref[g], blk) + bs_ref[:, g:g + 1])
        rows.append(jnp.concatenate(cols, axis=1))
    mixed = jnp.concatenate(rows, axis=0)
    y = (_gelu(u_raw) * mixed).astype(_BF16)
    gya_ref[0] = (jax.nn.sigmoid(ga_raw) * _dot(y, wao_ref[...])).astype(gya_ref.dtype)

    pos_col = jnp.broadcast_to(pos_ref[0].astype(_F32), (LANES, tm)).T
    rg = tm // ROPE_PACK
    lane = lax.broadcasted_iota(jnp.int32, (rg, LANES), 1)
    pos_packed = pos_col[(ROPE_PACK - 1) * rg:]
    for g in range(ROPE_PACK - 2, -1, -1):
        pos_packed = jnp.where(lane < (g + 1) * ROPE_HALF, pos_col[g * rg:(g + 1) * rg], pos_packed)
    ang = pos_packed * freq_ref[...]
    cos_p, sin_p = jnp.cos(ang), jnp.sin(ang)

    def unpack(tab):
        tiles = []
        shifted = lambda s: tab if s % LANES == 0 else pltpu.roll(tab, s % LANES, 1)
        for g in range(ROPE_PACK):
            lo = shifted(-g * ROPE_HALF)
            hi = shifted((1 - g) * ROPE_HALF)
            tiles.append(jnp.where(lane < ROPE_HALF, lo, hi))
        return jnp.concatenate(tiles, axis=0)

    cos_t = unpack(cos_p)
    sin_t = unpack(sin_p) * sign_ref[...]

    def rope(t):
        return t * cos_t + _rope_partner(t) * sin_t

    for h in range(N_HEADS):
        qh = qall[:, h * HEAD_SLAB:(h + 1) * HEAD_SLAB]
        r = lax.rsqrt(jnp.sum(qh * qh, axis=-1, keepdims=True) * (1.0 / QK_DIM) + EPS) * QSCALE_LOG2
        qn = qh * r * qg_ref[...]
        q_ref[0, h] = jnp.concatenate([qn[:, :LANES], rope(qn[:, LANES:]) + bias_ref[0:1, :]],
                                      axis=1).astype(q_ref.dtype)

    kpe_ss = jnp.sum(kpe * kpe, axis=-1, keepdims=True)
    kpe_rot = rope(kpe * kg_ref[:, LANES:])
    for h in range(N_HEADS):
        kn = kv[:, h * HEAD_SLAB:h * HEAD_SLAB + LANES]
        r = lax.rsqrt((jnp.sum(kn * kn, axis=-1, keepdims=True) + kpe_ss) * (1.0 / QK_DIM) + EPS)
        k_ref[0, h] = jnp.concatenate([kn * r * kg_ref[:, :LANES], kpe_rot * r + bias_ref[1:2, :]],
                                      axis=1).astype(k_ref.dtype)
        vt_ref[0, h, 0] = kv[:, h * HEAD_SLAB + LANES:(h + 1) * HEAD_SLAB].T.astype(vt_ref.dtype)


def _pre_call(x, pos3, consts):
    B, S, D = x.shape
    tm = TM_PRE
    nt = S // tm
    grid = (B, nt)
    in_specs = [
        pl.BlockSpec((1, tm, D), lambda b, i: (b, i, 0)),
        pl.BlockSpec((1, 1, tm), lambda b, i: (b, 0, i)),
    ] + [_const_spec(c.shape) for c in consts]
    out_shape = [
        jax.ShapeDtypeStruct((B, S, D), _BF16),
        jax.ShapeDtypeStruct((B, S, D), _BF16),
        jax.ShapeDtypeStruct((B, N_HEADS, S, HEAD_SLAB), _BF16),
        jax.ShapeDtypeStruct((B, N_HEADS, S, HEAD_SLAB), _BF16),
        jax.ShapeDtypeStruct((B, N_HEADS, nt, V_DIM, tm), _BF16),
    ]
    out_specs = [
        pl.BlockSpec((1, tm, D), lambda b, i: (b, i, 0)),
        pl.BlockSpec((1, tm, D), lambda b, i: (b, i, 0)),
        pl.BlockSpec((1, N_HEADS, tm, HEAD_SLAB), lambda b, i: (b, 0, i, 0)),
        pl.BlockSpec((1, N_HEADS, tm, HEAD_SLAB), lambda b, i: (b, 0, i, 0)),
        pl.BlockSpec((1, N_HEADS, 1, V_DIM, tm), lambda b, i: (b, 0, i, 0, 0)),
    ]
    return pl.pallas_call(
        _pre_kernel, grid=grid, in_specs=in_specs, out_specs=out_specs, out_shape=out_shape,
        compiler_params=pltpu.CompilerParams(
            dimension_semantics=("parallel", "parallel"), vmem_limit_bytes=V7X_VMEM_LIMIT),
        name="pre",
    )(x, pos3, *consts)


def _attn_kernel(bounded_ref, q_ref, k_ref, vt_ref, o_ref, qt_ref, st_ref, m_ref, l_ref, acc_ref):
    sub = TK // TM_PRE
    nk = k_ref.shape[2] // TK

    def values_t(j):
        return jnp.concatenate([vt_ref[0, 0, j * sub + s] for s in range(sub)], axis=1)

    def finish(acc, l_parts, rows):
        l = jnp.sum(l_parts, axis=0, keepdims=True)
        o_ref[0, rows, :] = (acc / l).T.astype(o_ref.dtype)

    def run_bounded():
        for t in range(q_ref.shape[2] // TQ_SUB):
            rows = slice(t * TQ_SUB, (t + 1) * TQ_SUB)
            qt = q_ref[0, 0, rows, :].T
            l = jnp.zeros((8, TQ_SUB), _F32)
            acc = jnp.zeros((V_DIM, TQ_SUB), _F32)
            for j in range(nk):
                p = jnp.exp2(_dot(k_ref[0, 0, j * TK:(j + 1) * TK, :], qt))
                l = l + jnp.sum(p.reshape(TK // 8, 8, TQ_SUB), axis=0)
                acc = acc + _dot(values_t(j), p.astype(_BF16))
            finish(acc, l, rows)

    def run_online():
        qt_ref[...] = q_ref[0, 0].T
        m_ref[...] = jnp.full(m_ref.shape, -jnp.inf, _F32)
        l_ref[...] = jnp.zeros(l_ref.shape, _F32)
        acc_ref[...] = jnp.zeros(acc_ref.shape, _F32)

        def scores(j, slot):
            kblk = k_ref[0, 0, pl.ds(pl.multiple_of(j * TK, TK), TK), :]
            st_ref[slot] = _dot(kblk, qt_ref[...])

        def online_step(j, slot):
            st = st_ref[slot]
            m_old = m_ref[...]
            m_new = jnp.maximum(m_old, jnp.max(st, axis=0, keepdims=True))
            p = jnp.exp2(st - m_new)
            alpha = jnp.exp2(m_old - m_new)
            l_ref[0:1, :] = alpha * l_ref[0:1, :] + jnp.sum(p, axis=0, keepdims=True)
            acc_ref[...] = alpha * acc_ref[...] + _dot(values_t(j), p.astype(_BF16))
            m_ref[...] = m_new

        scores(0, 0)

        def pair(jj, carry):
            j = 2 * jj
            scores(j + 1, 1)
            online_step(j, 0)
            scores(j + 2, 0)
            online_step(j + 1, 1)
            return carry

        lax.fori_loop(0, nk // 2 - 1, pair, 0)
        scores(nk - 1, 1)
        online_step(nk - 2, 0)
        online_step(nk - 1, 1)
        finish(acc_ref[...], l_ref[...], slice(None))

    lax.cond(bounded_ref[0] == 1, run_bounded, run_online)


def _attn_call(bounded, q, k, vt):
    B, H, S, _ = q.shape
    assert S % (2 * TK) == 0 and S % TQ == 0 and TK % TM_PRE == 0
    grid = (B, H, S // TQ)
    return pl.pallas_call(
        _attn_kernel, grid=grid,
        in_specs=[
            pl.BlockSpec(memory_space=pltpu.SMEM),
            pl.BlockSpec((1, 1, TQ, HEAD_SLAB), lambda b, h, i: (b, h, i, 0)),
            pl.BlockSpec((1, 1, S, HEAD_SLAB), lambda b, h, i: (b, h, 0, 0)),
            pl.BlockSpec((1, 1, S // TM_PRE, V_DIM, TM_PRE), lambda b, h, i: (b, h, 0, 0, 0)),
        ],
        out_specs=pl.BlockSpec((1, TQ, V_DIM), lambda b, h, i: (b, i, h)),
        out_shape=jax.ShapeDtypeStruct((B, S, H * V_DIM), _BF16),
        scratch_shapes=[
            pltpu.VMEM((HEAD_SLAB, TQ), _BF16),
            pltpu.VMEM((2, TK, TQ), _F32),
            pltpu.VMEM((1, TQ), _F32),
            pltpu.VMEM((8, TQ), _F32),
            pltpu.VMEM((V_DIM, TQ), _F32),
        ],
        compiler_params=pltpu.CompilerParams(
            dimension_semantics=("parallel", "parallel", "arbitrary"), vmem_limit_bytes=V7X_VMEM_LIMIT),
        name="attn",
    )(bounded, q, k, vt)


def _mix_ffn_kernel(x_ref, xp_ref, xn_ref, o_ref, op_ref, on_ref, gya_ref, gyap_ref, gyan_ref,
                    gb_ref, gbp_ref, gbn_ref, wbo_ref, wout_ref, g2_ref, wup_ref, cw_ref, cb_ref, wdown_ref,
                    out_ref, ext_ref, act_ref):
    i = pl.program_id(1)
    nt = pl.num_programs(1)
    tm = x_ref.shape[1]

    def rows3(prev, main, nxt):
        return jnp.concatenate([prev[0], main[0], nxt[0]], axis=0)

    yb = _dot(rows3(op_ref, o_ref, on_ref), wbo_ref[...])
    merged = (rows3(gyap_ref, gya_ref, gyan_ref).astype(_F32)
              + rows3(gbp_ref, gb_ref, gbn_ref).astype(_F32) * yb).astype(_BF16)
    h_ext = rows3(xp_ref, x_ref, xn_ref) + _dot(merged, wout_ref[...])
    ext_ref[...] = _rms(h_ext, g2_ref[...]).astype(_BF16)
    out_ref[0] = h_ext[HALO:HALO + tm]

    keep_prev = (i > 0).astype(_F32)
    keep_next = (i < nt - 1).astype(_F32)

    def up_proj(c):
        ups = []
        for col in (c * FF_CHUNK, D_FF + c * FF_CHUNK):
            up = _dot(ext_ref[...], wup_ref[:, col:col + FF_CHUNK])
            ups.append(jnp.concatenate(
                [up[:HALO] * keep_prev, up[HALO:HALO + tm], up[HALO + tm:] * keep_next], axis=0))
        return ups

    def conv(up, col):
        cols = slice(col, col + FF_CHUNK)
        rows = up.shape[0]
        return (pltpu.roll(up, 1, 0)[HALO:HALO + tm] * cw_ref[0:1, cols]
                + up[HALO:HALO + tm] * cw_ref[1:2, cols]
                + pltpu.roll(up, rows - 1, 0)[HALO:HALO + tm] * cw_ref[2:3, cols] + cb_ref[:, cols])

    n_chunks = D_FF // FF_CHUNK
    up_next = up_proj(0)
    for c in range(n_chunks):
        up_val, up_gate = up_next
        if c + 1 < n_chunks:
            up_next = up_proj(c + 1)
        act_ref[:, c * FF_CHUNK:(c + 1) * FF_CHUNK] = (
            jax.nn.silu(conv(up_gate, D_FF + c * FF_CHUNK)) * conv(up_val, c * FF_CHUNK)).astype(_BF16)
    out_ref[0] += _dot(act_ref[...], wdown_ref[...])


def _mix_ffn_call(x, o, gya, gb, wbo, wout, g2, wup, cw, cb, wdown):
    B, S, D = x.shape
    tm = TM_FFN
    nt = S // tm
    per = tm // HALO
    main = pl.BlockSpec((1, tm, D), lambda b, i: (b, i, 0))
    prev = pl.BlockSpec((1, HALO, D), lambda b, i: (b, jnp.maximum(i * per - 1, 0), 0))
    nxt = pl.BlockSpec((1, HALO, D), lambda b, i: (b, jnp.minimum((i + 1) * per, S // HALO - 1), 0))
    consts = (wbo, wout, g2, wup, cw, cb, wdown)
    return pl.pallas_call(
        _mix_ffn_kernel, grid=(B, nt),
        in_specs=[main, prev, nxt] * 4 + [_const_spec(c.shape) for c in consts],
        out_specs=main,
        out_shape=jax.ShapeDtypeStruct((B, S, D), _F32),
        scratch_shapes=[
            pltpu.VMEM((tm + 2 * HALO, D), _BF16),
            pltpu.VMEM((tm, D_FF), _BF16),
        ],
        compiler_params=pltpu.CompilerParams(
            dimension_semantics=("parallel", "parallel"), vmem_limit_bytes=V7X_VMEM_LIMIT),
        name="mix_ffn",
    )(x, x, x, o, o, o, gya, gya, gya, gb, gb, gb, *consts)


def _pad_head_cols(w):
    lead = w.shape[:-1]
    w = w.reshape(lead + (N_HEADS, QK_DIM))
    w = jnp.pad(w, [(0, 0)] * len(lead) + [(0, 0), (0, HEAD_SLAB - QK_DIM)])
    return w.reshape(lead + (N_HEADS * HEAD_SLAB,))


def kernel(x, positions, norm1_g, w_in, v_ln_g, v_ln_b, w_s, b_s, w_a_o, q_norm_g, w_uq, kv_norm_g, w_ukv,
           q_head_g, k_head_g, w_b_o, w_out, norm2_g, w_up, conv_w, conv_b, w_down):
    B, S, D = x.shape
    bf = lambda a: a.astype(_BF16)
    row = lambda a: a.reshape(1, -1).astype(_F32)

    c0, c1, c2, c3, c4 = D_A, 2 * D_A, 2 * D_A + Q_RANK, 2 * D_A + Q_RANK + KV_RANK, 2 * D_A + Q_RANK + KV_RANK + QK_ROPE
    w_kr = jnp.pad(w_in[:, c3:c4], ((0, 0), (0, LANES - QK_ROPE)))
    pad_g = lambda g: jnp.pad(g, (0, HEAD_SLAB - QK_DIM)).reshape(1, HEAD_SLAB).astype(_F32)
    inv_freq = ROPE_THETA ** (-jnp.arange(0, QK_ROPE, 2, dtype=_F32) / QK_ROPE)
    zeros = jnp.zeros((LANES - QK_ROPE,), _F32)
    freq = jnp.tile(inv_freq, ROPE_PACK).reshape(1, LANES)
    sign = jnp.concatenate([-jnp.ones((ROPE_HALF,), _F32), jnp.ones((ROPE_HALF,), _F32), zeros]).reshape(1, LANES)

    bound = (QSCALE_LOG2 * QK_DIM * (1.0 + 2.0 ** -6)) * jnp.max(jnp.abs(q_head_g)) * jnp.max(jnp.abs(k_head_g))
    bounded = bound <= SCORE_BOUND_MAX
    bias = jnp.zeros((2, LANES), _F32).at[0, BIAS_LANE].set(1.0).at[1, BIAS_LANE].set(
        jnp.where(bounded, -bound, 0.0).astype(_F32))

    consts = [
        row(norm1_g), bf(w_in[:, :c0]), bf(w_in[:, c0:c1]), bf(w_in[:, c1:c2]), bf(w_in[:, c2:c3]), bf(w_kr),
        bf(w_in[:, c4:c4 + D]), bf(w_in[:, c4 + D:]),
        row(v_ln_g), row(v_ln_b), bf(w_s), jnp.transpose(b_s).astype(_F32), bf(w_a_o),
        row(q_norm_g), bf(_pad_head_cols(w_uq)), row(kv_norm_g), bf(w_ukv),
        pad_g(q_head_g), pad_g(k_head_g), freq, sign, bias,
    ]
    gya, gb, q, k, vt = _pre_call(x, positions.reshape(B, 1, S), consts)
    o = _attn_call(bounded.astype(jnp.int32).reshape(1), q, k, vt)
    return _mix_ffn_call(x, o, gya, gb, bf(w_b_o), bf(w_out), row(norm2_g),
                         bf(w_up), conv_w.astype(_F32), row(conv_b), bf(w_down))
```

```python
import math

import jax
import jax.numpy as jnp
from jax import lax
from jax.experimental import pallas as pl
from jax.experimental.pallas import tpu as pltpu

D_MODEL = 1024
CHUNK = 128
A_GROUPS = 8
A_GROUP_DIM = 128
D_A = A_GROUPS * A_GROUP_DIM
N_HEADS = 8
Q_RANK = 384
KV_RANK = 256
QK_NOPE = 128
QK_ROPE = 64
V_DIM = 128
QK_DIM = QK_NOPE + QK_ROPE
ROPE_THETA = 10000.0
D_FF = 2816
EPS = 1e-6

IN_V = D_A
IN_CQ = 2 * D_A
IN_CKV = IN_CQ + Q_RANK
IN_KR = IN_CKV + KV_RANK
IN_GA = IN_KR + QK_ROPE
IN_GB = IN_GA + D_MODEL
IN_END = IN_GB + D_MODEL

LANES = 128
HEAD_SLAB = 2 * LANES
ROPE_HALF = QK_ROPE // 2
ROPE_PACK = LANES // ROPE_HALF
BIAS_LANE = QK_ROPE
QSCALE_LOG2 = math.log2(math.e) / math.sqrt(QK_DIM)
SCORE_BOUND_MAX = 40.0
V7X_VMEM_LIMIT = 56 * 1024 * 1024

TM_PRE = 512
TQ = 2048
TK = 1024
TM_FFN = 512
FF_CHUNK = 256
HALO = 16

_BF16 = jnp.bfloat16
_F32 = jnp.float32


def _dot(a, b):
    return jnp.dot(a, b, preferred_element_type=_F32)


def _gelu(x):
    c = 2.0 * math.sqrt(2.0 / math.pi) * math.log2(math.e)
    z = x * (x * x * (-0.044715 * c) - c)
    return x / (1.0 + jnp.exp2(z))


def _rms(x, g):
    return x * lax.rsqrt(jnp.mean(x * x, axis=-1, keepdims=True) + EPS) * g


def _rope_partner(t):
    lane = lax.broadcasted_iota(jnp.int32, t.shape, 1)
    return jnp.where(lane < ROPE_HALF, pltpu.roll(t, LANES - ROPE_HALF, 1), pltpu.roll(t, ROPE_HALF, 1))


def _const_spec(shape):
    nd = len(shape)
    return pl.BlockSpec(shape, lambda *_: (0,) * nd, pipeline_mode=pl.Buffered(1))


def _pre_kernel(x_ref, pos_ref, g1_ref, wint_ref,
                lng_ref, lnb_ref, ws_ref, bs_ref, wao_ref, qng_ref, wuq_ref, kvng_ref, wukv_ref,
                qg_ref, kg_ref, freq_ref, sign_ref, bias_ref,
                gya_ref, gb_ref, q_ref, k_ref, vt_ref):
    tm = x_ref.shape[1]
    xn = _rms(x_ref[0], g1_ref[...]).astype(_BF16)

    def in_proj(lo, hi):
        return lax.dot_general(xn, wint_ref[lo:hi, :], (((1,), (1,)), ((), ())), preferred_element_type=_F32)

    cq = in_proj(IN_CQ, IN_CKV)
    ckv = in_proj(IN_CKV, IN_KR)
    kpe_tile = in_proj(IN_KR, IN_KR + LANES)
    kpe = jnp.where(lax.broadcasted_iota(jnp.int32, kpe_tile.shape, 1) < QK_ROPE, kpe_tile, 0.0)
    v_raw = in_proj(IN_V, IN_CQ)
    qall = _dot(_rms(cq, qng_ref[...]).astype(_BF16), wuq_ref[...])
    kv = _dot(_rms(ckv, kvng_ref[...]).astype(_BF16), wukv_ref[...])
    gb_ref[0] = jax.nn.sigmoid(in_proj(IN_GB, IN_END)).astype(gb_ref.dtype)
    u_raw = in_proj(0, IN_V)
    ga_raw = in_proj(IN_GA, IN_GB)

    gv = _gelu(v_raw)
    mu = jnp.mean(gv, axis=-1, keepdims=True)
    vc = gv - mu
    vln = (vc * lax.rsqrt(jnp.mean(vc * vc, axis=-1, keepdims=True) + EPS) * lng_ref[...]
           + lnb_ref[...]).astype(_BF16)
    rows = []
    for c in range(tm // CHUNK):
        cols = []
        for g in range(A_GROUPS):
            blk = vln[c * CHUNK:(c + 1) * CHUNK, g * A_GROUP_DIM:(g + 1) * A_GROUP_DIM]
            cols.append(_dot(ws_ref[g], blk) + bs_ref[:, g:g + 1])
        rows.append(jnp.concatenate(cols, axis=1))
    mixed = jnp.concatenate(rows, axis=0)
    y = (_gelu(u_raw) * mixed).astype(_BF16)
    gya_ref[0] = (jax.nn.sigmoid(ga_raw) * _dot(y, wao_ref[...])).astype(gya_ref.dtype)

    pos_col = jnp.broadcast_to(pos_ref[0].astype(_F32), (LANES, tm)).T
    rg = tm // ROPE_PACK
    lane = lax.broadcasted_iota(jnp.int32, (rg, LANES), 1)
    pos_packed = pos_col[(ROPE_PACK - 1) * rg:]
    for g in range(ROPE_PACK - 2, -1, -1):
        pos_packed = jnp.where(lane < (g + 1) * ROPE_HALF, pos_col[g * rg:(g + 1) * rg], pos_packed)
    ang = pos_packed * freq_ref[...]
    cos_p, sin_p = jnp.cos(ang), jnp.sin(ang)

    def unpack(tab):
        tiles = []
        shifted = lambda s: tab if s % LANES == 0 else pltpu.roll(tab, s % LANES, 1)
        for g in range(ROPE_PACK):
            lo = shifted(-g * ROPE_HALF)
            hi = shifted((1 - g) * ROPE_HALF)
            tiles.append(jnp.where(lane < ROPE_HALF, lo, hi))
        return jnp.concatenate(tiles, axis=0)

    cos_t = unpack(cos_p)
    sin_t = unpack(sin_p) * sign_ref[...]

    def rope(t):
        return t * cos_t + _rope_partner(t) * sin_t

    for h in range(N_HEADS):
        qh = qall[:, h * HEAD_SLAB:(h + 1) * HEAD_SLAB]
        r = lax.rsqrt(jnp.sum(qh * qh, axis=-1, keepdims=True) * (1.0 / QK_DIM) + EPS) * QSCALE_LOG2
        qn = qh * r * qg_ref[...]
        q_ref[0, h] = jnp.concatenate([qn[:, :LANES], rope(qn[:, LANES:]) + bias_ref[0:1, :]],
                                      axis=1).astype(q_ref.dtype)

    kpe_ss = jnp.sum(kpe * kpe, axis=-1, keepdims=True)
    kpe_rot = rope(kpe * kg_ref[:, LANES:])
    for h in range(N_HEADS):
        kn = kv[:, h * HEAD_SLAB:h * HEAD_SLAB + LANES]
        r = lax.rsqrt((jnp.sum(kn * kn, axis=-1, keepdims=True) + kpe_ss) * (1.0 / QK_DIM) + EPS)
        k_ref[0, h] = jnp.concatenate([kn * r * kg_ref[:, :LANES], kpe_rot * r + bias_ref[1:2, :]],
                                      axis=1).astype(k_ref.dtype)
        vt_ref[0, h, 0] = kv[:, h * HEAD_SLAB + LANES:(h + 1) * HEAD_SLAB].T.astype(vt_ref.dtype)


def _pre_call(x, pos3, consts):
    B, S, D = x.shape
    tm = TM_PRE
    nt = S // tm
    grid = (B, nt)
    in_specs = [
        pl.BlockSpec((1, tm, D), lambda b, i: (b, i, 0)),
        pl.BlockSpec((1, 1, tm), lambda b, i: (b, 0, i)),
    ] + [_const_spec(c.shape) for c in consts]
    out_shape = [
        jax.ShapeDtypeStruct((B, S, D), _BF16),
        jax.ShapeDtypeStruct((B, S, D), _BF16),
        jax.ShapeDtypeStruct((B, N_HEADS, S, HEAD_SLAB), _BF16),
        jax.ShapeDtypeStruct((B, N_HEADS, S, HEAD_SLAB), _BF16),
        jax.ShapeDtypeStruct((B, N_HEADS, nt, V_DIM, tm), _BF16),
    ]
    out_specs = [
        pl.BlockSpec((1, tm, D), lambda b, i: (b, i, 0)),
        pl.BlockSpec((1, tm, D), lambda b, i: (b, i, 0)),
        pl.BlockSpec((1, N_HEADS, tm, HEAD_SLAB), lambda b, i: (b, 0, i, 0)),
        pl.BlockSpec((1, N_HEADS, tm, HEAD_SLAB), lambda b, i: (b, 0, i, 0)),
        pl.BlockSpec((1, N_HEADS, 1, V_DIM, tm), lambda b, i: (b, 0, i, 0, 0)),
    ]
    return pl.pallas_call(
        _pre_kernel, grid=grid, in_specs=in_specs, out_specs=out_specs, out_shape=out_shape,
        compiler_params=pltpu.CompilerParams(
            dimension_semantics=("parallel", "parallel"), vmem_limit_bytes=V7X_VMEM_LIMIT),
        name="pre",
    )(x, pos3, *consts)


def _attn_kernel(bounded_ref, q_ref, k_ref, vt_ref, o_ref, qt_ref, st_ref, m_ref, l_ref, acc_ref):
    sub = TK // TM_PRE
    nk = k_ref.shape[2] // TK
    qt_ref[...] = q_ref[0, 0].T

    def scores(j, slot):
        start = j * TK if isinstance(j, int) else pl.multiple_of(j * TK, TK)
        kblk = k_ref[0, 0, pl.ds(start, TK), :]
        st_ref[slot] = _dot(kblk, qt_ref[...])

    def values_t(j):
        return jnp.concatenate([vt_ref[0, 0, j * sub + s] for s in range(sub)], axis=1)

    def online_step(j, slot):
        st = st_ref[slot]
        m_old = m_ref[...]
        m_new = jnp.maximum(m_old, jnp.max(st, axis=0, keepdims=True))
        p = jnp.exp2(st - m_new)
        alpha = jnp.exp2(m_old - m_new)
        l_ref[0:1, :] = alpha * l_ref[0:1, :] + jnp.sum(p, axis=0, keepdims=True)
        acc_ref[...] = alpha * acc_ref[...] + _dot(values_t(j), p.astype(_BF16))
        m_ref[...] = m_new

    def run_bounded():
        l = jnp.zeros(l_ref.shape, _F32)
        acc = jnp.zeros(acc_ref.shape, _F32)
        for j in range(nk):
            p = jnp.exp2(_dot(k_ref[0, 0, j * TK:(j + 1) * TK, :], qt_ref[...]))
            l = l + jnp.sum(p.reshape(TK // 8, 8, p.shape[1]), axis=0)
            acc = acc + _dot(values_t(j), p.astype(_BF16))
        l_ref[...] = l
        acc_ref[...] = acc

    def run_online():
        scores(0, 0)

        def pair(jj, carry):
            j = 2 * jj
            scores(j + 1, 1)
            online_step(j, 0)
            scores(j + 2, 0)
            online_step(j + 1, 1)
            return carry

        lax.fori_loop(0, nk // 2 - 1, pair, 0)
        scores(nk - 1, 1)
        online_step(nk - 2, 0)
        online_step(nk - 1, 1)

    m_ref[...] = jnp.full(m_ref.shape, -jnp.inf, _F32)
    l_ref[...] = jnp.zeros(l_ref.shape, _F32)
    acc_ref[...] = jnp.zeros(acc_ref.shape, _F32)
    lax.cond(bounded_ref[0] == 1, run_bounded, run_online)
    l = jnp.sum(l_ref[...], axis=0, keepdims=True)
    o_ref[0] = (acc_ref[...] / l).T.astype(o_ref.dtype)


def _attn_call(bounded, q, k, vt):
    B, H, S, _ = q.shape
    assert S % (2 * TK) == 0 and S % TQ == 0 and TK % TM_PRE == 0
    grid = (B, H, S // TQ)
    return pl.pallas_call(
        _attn_kernel, grid=grid,
        in_specs=[
            pl.BlockSpec(memory_space=pltpu.SMEM),
            pl.BlockSpec((1, 1, TQ, HEAD_SLAB), lambda b, h, i: (b, h, i, 0)),
            pl.BlockSpec((1, 1, S, HEAD_SLAB), lambda b, h, i: (b, h, 0, 0)),
            pl.BlockSpec((1, 1, S // TM_PRE, V_DIM, TM_PRE), lambda b, h, i: (b, h, 0, 0, 0)),
        ],
        out_specs=pl.BlockSpec((1, TQ, V_DIM), lambda b, h, i: (b, i, h)),
        out_shape=jax.ShapeDtypeStruct((B, S, H * V_DIM), _BF16),
        scratch_shapes=[
            pltpu.VMEM((HEAD_SLAB, TQ), _BF16),
            pltpu.VMEM((2, TK, TQ), _F32),
            pltpu.VMEM((1, TQ), _F32),
            pltpu.VMEM((8, TQ), _F32),
            pltpu.VMEM((V_DIM, TQ), _F32),
        ],
        compiler_params=pltpu.CompilerParams(
            dimension_semantics=("parallel", "parallel", "arbitrary"), vmem_limit_bytes=V7X_VMEM_LIMIT),
        name="attn",
    )(bounded, q, k, vt)


def _mix_ffn_kernel(x_ref, xp_ref, xn_ref, o_ref, op_ref, on_ref, gya_ref, gyap_ref, gyan_ref,
                    gb_ref, gbp_ref, gbn_ref, wbo_ref, wout_ref, g2_ref, wup_ref, cw_ref, cb_ref, wdown_ref,
                    out_ref, ext_ref, act_ref):
    i = pl.program_id(1)
    nt = pl.num_programs(1)
    tm = x_ref.shape[1]

    def rows3(prev, main, nxt):
        return jnp.concatenate([prev[0], main[0], nxt[0]], axis=0)

    yb = _dot(rows3(op_ref, o_ref, on_ref), wbo_ref[...])
    merged = (rows3(gyap_ref, gya_ref, gyan_ref).astype(_F32)
              + rows3(gbp_ref, gb_ref, gbn_ref).astype(_F32) * yb).astype(_BF16)
    h_ext = rows3(xp_ref, x_ref, xn_ref) + _dot(merged, wout_ref[...])
    ext_ref[...] = _rms(h_ext, g2_ref[...]).astype(_BF16)
    out_ref[0] = h_ext[HALO:HALO + tm]

    keep_prev = (i > 0).astype(_F32)
    keep_next = (i < nt - 1).astype(_F32)

    def up_proj(c):
        ups = []
        for col in (c * FF_CHUNK, D_FF + c * FF_CHUNK):
            up = _dot(ext_ref[...], wup_ref[:, col:col + FF_CHUNK])
            ups.append(jnp.concatenate(
                [up[:HALO] * keep_prev, up[HALO:HALO + tm], up[HALO + tm:] * keep_next], axis=0))
        return ups

    def conv(up, col):
        cols = slice(col, col + FF_CHUNK)
        rows = up.shape[0]
        return (pltpu.roll(up, 1, 0)[HALO:HALO + tm] * cw_ref[0:1, cols]
                + up[HALO:HALO + tm] * cw_ref[1:2, cols]
                + pltpu.roll(up, rows - 1, 0)[HALO:HALO + tm] * cw_ref[2:3, cols] + cb_ref[:, cols])

    n_chunks = D_FF // FF_CHUNK
    up_next = up_proj(0)
    for c in range(n_chunks):
        up_val, up_gate = up_next
        if c + 1 < n_chunks:
            up_next = up_proj(c + 1)
        act_ref[:, c * FF_CHUNK:(c + 1) * FF_CHUNK] = (
            jax.nn.silu(conv(up_gate, D_FF + c * FF_CHUNK)) * conv(up_val, c * FF_CHUNK)).astype(_BF16)
    out_ref[0] += _dot(act_ref[...], wdown_ref[...])


def _mix_ffn_call(x, o, gya, gb, wbo, wout, g2, wup, cw, cb, wdown):
    B, S, D = x.shape
    tm = TM_FFN
    nt = S // tm
    per = tm // HALO
    main = pl.BlockSpec((1, tm, D), lambda b, i: (b, i, 0))
    prev = pl.BlockSpec((1, HALO, D), lambda b, i: (b, jnp.maximum(i * per - 1, 0), 0))
    nxt = pl.BlockSpec((1, HALO, D), lambda b, i: (b, jnp.minimum((i + 1) * per, S // HALO - 1), 0))
    consts = (wbo, wout, g2, wup, cw, cb, wdown)
    return pl.pallas_call(
        _mix_ffn_kernel, grid=(B, nt),
        in_specs=[main, prev, nxt] * 4 + [_const_spec(c.shape) for c in consts],
        out_specs=main,
        out_shape=jax.ShapeDtypeStruct((B, S, D), _F32),
        scratch_shapes=[
            pltpu.VMEM((tm + 2 * HALO, D), _BF16),
            pltpu.VMEM((tm, D_FF), _BF16),
        ],
        compiler_params=pltpu.CompilerParams(
            dimension_semantics=("parallel", "parallel"), vmem_limit_bytes=V7X_VMEM_LIMIT),
        name="mix_ffn",
    )(x, x, x, o, o, o, gya, gya, gya, gb, gb, gb, *consts)


def _pad_head_cols(w):
    lead = w.shape[:-1]
    w = w.reshape(lead + (N_HEADS, QK_DIM))
    w = jnp.pad(w, [(0, 0)] * len(lead) + [(0, 0), (0, HEAD_SLAB - QK_DIM)])
    return w.reshape(lead + (N_HEADS * HEAD_SLAB,))


def kernel(x, positions, norm1_g, w_in, v_ln_g, v_ln_b, w_s, b_s, w_a_o, q_norm_g, w_uq, kv_norm_g, w_ukv,
           q_head_g, k_head_g, w_b_o, w_out, norm2_g, w_up, conv_w, conv_b, w_down):
    B, S, D = x.shape
    bf = lambda a: a.astype(_BF16)
    row = lambda a: a.reshape(1, -1).astype(_F32)

    pad_g = lambda g: jnp.pad(g, (0, HEAD_SLAB - QK_DIM)).reshape(1, HEAD_SLAB).astype(_F32)
    inv_freq = ROPE_THETA ** (-jnp.arange(0, QK_ROPE, 2, dtype=_F32) / QK_ROPE)
    zeros = jnp.zeros((LANES - QK_ROPE,), _F32)
    freq = jnp.tile(inv_freq, ROPE_PACK).reshape(1, LANES)
    sign = jnp.concatenate([-jnp.ones((ROPE_HALF,), _F32), jnp.ones((ROPE_HALF,), _F32), zeros]).reshape(1, LANES)

    bound = (QSCALE_LOG2 * QK_DIM * (1.0 + 2.0 ** -6)) * jnp.max(jnp.abs(q_head_g)) * jnp.max(jnp.abs(k_head_g))
    bounded = bound <= SCORE_BOUND_MAX
    bias = jnp.zeros((2, LANES), _F32).at[0, BIAS_LANE].set(1.0).at[1, BIAS_LANE].set(
        jnp.where(bounded, -bound, 0.0).astype(_F32))

    consts = [
        row(norm1_g), bf(w_in.T),
        row(v_ln_g), row(v_ln_b), bf(w_s), jnp.transpose(b_s).astype(_F32), bf(w_a_o),
        row(q_norm_g), bf(_pad_head_cols(w_uq)), row(kv_norm_g), bf(w_ukv),
        pad_g(q_head_g), pad_g(k_head_g), freq, sign, bias,
    ]
    gya, gb, q, k, vt = _pre_call(x, positions.reshape(B, 1, S), consts)
    o = _attn_call(bounded.astype(jnp.int32).reshape(1), q, k, vt)
    return _mix_ffn_call(x, o, gya, gb, bf(w_b_o), bf(w_out), row(norm2_g),
                         bf(w_up), conv_w.astype(_F32), row(conv_b), bf(w_down))
```

```python
import math

import jax
import jax.numpy as jnp
from jax import lax
from jax.experimental import pallas as pl
from jax.experimental.pallas import tpu as pltpu

D_MODEL = 1024
CHUNK = 128
A_GROUPS = 8
A_GROUP_DIM = 128
D_A = A_GROUPS * A_GROUP_DIM
N_HEADS = 8
Q_RANK = 384
KV_RANK = 256
QK_NOPE = 128
QK_ROPE = 64
V_DIM = 128
QK_DIM = QK_NOPE + QK_ROPE
ROPE_THETA = 10000.0
D_FF = 2816
EPS = 1e-6

IN_V = D_A
IN_CQ = 2 * D_A
IN_CKV = IN_CQ + Q_RANK
IN_KR = IN_CKV + KV_RANK
IN_GA = IN_KR + QK_ROPE
IN_GB = IN_GA + D_MODEL
IN_END = IN_GB + D_MODEL

LANES = 128
BF16_SUBLANES = 16
HEAD_SLAB = 2 * LANES
ROPE_HALF = QK_ROPE // 2
ROPE_PACK = LANES // ROPE_HALF
BIAS_LANE = QK_ROPE
QSCALE_LOG2 = math.log2(math.e) / math.sqrt(QK_DIM)
SCORE_BOUND_MAX = 40.0
V7X_VMEM_LIMIT = 56 * 1024 * 1024

TM_PRE = 512
TQ = 2048
TK = 1024
TM_FFN = 512
FF_CHUNK = 256
HALO = 16

_BF16 = jnp.bfloat16
_F32 = jnp.float32


def _dot(a, b):
    return jnp.dot(a, b, preferred_element_type=_F32)


def _gelu(x):
    c = 2.0 * math.sqrt(2.0 / math.pi) * math.log2(math.e)
    z = x * (x * x * (-0.044715 * c) - c)
    return x / (1.0 + jnp.exp2(z))


def _rms(x, g):
    return x * lax.rsqrt(jnp.mean(x * x, axis=-1, keepdims=True) + EPS) * g


def _rope_partner(t):
    lane = lax.broadcasted_iota(jnp.int32, t.shape, 1)
    return jnp.where(lane < ROPE_HALF, pltpu.roll(t, LANES - ROPE_HALF, 1), pltpu.roll(t, ROPE_HALF, 1))


def _const_spec(shape):
    nd = len(shape)
    return pl.BlockSpec(shape, lambda *_: (0,) * nd, pipeline_mode=pl.Buffered(1))


def _pre_kernel(x_ref, pos_ref, g1_ref, wint_ref,
                lng_ref, lnb_ref, ws_ref, bs_ref, wao_ref, qng_ref, wuq_ref, kvng_ref, wukv_ref,
                qg_ref, kg_ref, freq_ref, sign_ref, bias_ref,
                gya_ref, gb_ref, q_ref, k_ref, vt_ref):
    tm = x_ref.shape[1]
    xn = _rms(x_ref[0], g1_ref[...]).astype(_BF16)

    def in_proj(lo, hi):
        return lax.dot_general(xn, wint_ref[lo:hi, :], (((1,), (1,)), ((), ())), preferred_element_type=_F32)

    cq = in_proj(IN_CQ, IN_CKV)
    ckv = in_proj(IN_CKV, IN_KR)
    kpe_tile = in_proj(IN_KR, IN_KR + LANES)
    kpe = jnp.where(lax.broadcasted_iota(jnp.int32, kpe_tile.shape, 1) < QK_ROPE, kpe_tile, 0.0)
    v_raw = in_proj(IN_V, IN_CQ)
    qall = _dot(_rms(cq, qng_ref[...]).astype(_BF16), wuq_ref[...])
    kv = _dot(_rms(ckv, kvng_ref[...]).astype(_BF16), wukv_ref[...])
    gb_ref[0] = jax.nn.sigmoid(in_proj(IN_GB, IN_END)).astype(gb_ref.dtype)
    u_raw = in_proj(0, IN_V)
    ga_raw = in_proj(IN_GA, IN_GB)

    gv = _gelu(v_raw)
    mu = jnp.mean(gv, axis=-1, keepdims=True)
    vc = gv - mu
    vln = (vc * lax.rsqrt(jnp.mean(vc * vc, axis=-1, keepdims=True) + EPS) * lng_ref[...]
           + lnb_ref[...]).astype(_BF16)
    rows = []
    for c in range(tm // CHUNK):
        cols = []
        for g in range(A_GROUPS):
            blk = vln[c * CHUNK:(c + 1) * CHUNK, g * A_GROUP_DIM:(g + 1) * A_GROUP_DIM]
            cols.append(_dot(ws_ref[g], blk) + bs_ref[:, g:g + 1])
        rows.append(jnp.concatenate(cols, axis=1))
    mixed = jnp.concatenate(rows, axis=0)
    y = (_gelu(u_raw) * mixed).astype(_BF16)
    gya_ref[0] = (jax.nn.sigmoid(ga_raw) * _dot(y, wao_ref[...])).astype(gya_ref.dtype)

    pos_col = jnp.broadcast_to(pos_ref[0].astype(_F32), (LANES, tm)).T
    rg = tm // ROPE_PACK
    lane = lax.broadcasted_iota(jnp.int32, (rg, LANES), 1)
    pos_packed = pos_col[(ROPE_PACK - 1) * rg:]
    for g in range(ROPE_PACK - 2, -1, -1):
        pos_packed = jnp.where(lane < (g + 1) * ROPE_HALF, pos_col[g * rg:(g + 1) * rg], pos_packed)
    ang = pos_packed * freq_ref[...]
    cos_p, sin_p = jnp.cos(ang), jnp.sin(ang)

    def unpack(tab):
        tiles = []
        shifted = lambda s: tab if s % LANES == 0 else pltpu.roll(tab, s % LANES, 1)
        for g in range(ROPE_PACK):
            lo = shifted(-g * ROPE_HALF)
            hi = shifted((1 - g) * ROPE_HALF)
            tiles.append(jnp.where(lane < ROPE_HALF, lo, hi))
        return jnp.concatenate(tiles, axis=0)

    cos_t = unpack(cos_p)
    sin_t = unpack(sin_p) * sign_ref[...]

    def rope(t):
        return t * cos_t + _rope_partner(t) * sin_t

    for h in range(N_HEADS):
        qh = qall[:, h * HEAD_SLAB:(h + 1) * HEAD_SLAB]
        r = lax.rsqrt(jnp.sum(qh * qh, axis=-1, keepdims=True) * (1.0 / QK_DIM) + EPS) * QSCALE_LOG2
        qn = qh * r * qg_ref[...]
        q_ref[0, h] = jnp.concatenate([qn[:, :LANES], rope(qn[:, LANES:]) + bias_ref[0:1, :]],
                                      axis=1).astype(q_ref.dtype)

    kpe_ss = jnp.sum(kpe * kpe, axis=-1, keepdims=True)
    kpe_rot = rope(kpe * kg_ref[:, LANES:])
    for h in range(N_HEADS):
        kn = kv[:, h * HEAD_SLAB:h * HEAD_SLAB + LANES]
        r = lax.rsqrt((jnp.sum(kn * kn, axis=-1, keepdims=True) + kpe_ss) * (1.0 / QK_DIM) + EPS)
        k_ref[0, h] = jnp.concatenate([kn * r * kg_ref[:, :LANES], kpe_rot * r + bias_ref[1:2, :]],
                                      axis=1).astype(k_ref.dtype)
        vt_ref[0, h, 0] = kv[:, h * HEAD_SLAB + LANES:(h + 1) * HEAD_SLAB].T.astype(vt_ref.dtype)


def _pre_call(x, pos3, consts):
    B, S, D = x.shape
    tm = TM_PRE
    nt = S // tm
    grid = (B, nt)
    in_specs = [
        pl.BlockSpec((1, tm, D), lambda b, i: (b, i, 0)),
        pl.BlockSpec((1, 1, tm), lambda b, i: (b, 0, i)),
    ] + [_const_spec(c.shape) for c in consts]
    out_shape = [
        jax.ShapeDtypeStruct((B, S, D), _BF16),
        jax.ShapeDtypeStruct((B, S, D), _BF16),
        jax.ShapeDtypeStruct((B, N_HEADS, S, HEAD_SLAB), _BF16),
        jax.ShapeDtypeStruct((B, N_HEADS, S, HEAD_SLAB), _BF16),
        jax.ShapeDtypeStruct((B, N_HEADS, nt, V_DIM, tm), _BF16),
    ]
    out_specs = [
        pl.BlockSpec((1, tm, D), lambda b, i: (b, i, 0)),
        pl.BlockSpec((1, tm, D), lambda b, i: (b, i, 0)),
        pl.BlockSpec((1, N_HEADS, tm, HEAD_SLAB), lambda b, i: (b, 0, i, 0)),
        pl.BlockSpec((1, N_HEADS, tm, HEAD_SLAB), lambda b, i: (b, 0, i, 0)),
        pl.BlockSpec((1, N_HEADS, 1, V_DIM, tm), lambda b, i: (b, 0, i, 0, 0)),
    ]
    return pl.pallas_call(
        _pre_kernel, grid=grid, in_specs=in_specs, out_specs=out_specs, out_shape=out_shape,
        compiler_params=pltpu.CompilerParams(
            dimension_semantics=("parallel", "parallel"), vmem_limit_bytes=V7X_VMEM_LIMIT),
        name="pre",
    )(x, pos3, *consts)


def _attn_kernel(bounded_ref, q_ref, k_ref, vt_ref, wup_ref, wdown_ref, wbo_ref, wout_ref,
                 o_ref, wup_bf_ref, wdown_bf_ref, wbo_bf_ref, wout_bf_ref,
                 qt_ref, st_ref, m_ref, l_ref, acc_ref):
    for src, dst in ((wup_ref, wup_bf_ref), (wdown_ref, wdown_bf_ref), (wbo_ref, wbo_bf_ref),
                     (wout_ref, wout_bf_ref)):
        dst[...] = src[...].astype(dst.dtype)

    sub = TK // TM_PRE
    nk = k_ref.shape[2] // TK
    qt_ref[...] = q_ref[0, 0].T

    def scores(j, slot):
        start = j * TK if isinstance(j, int) else pl.multiple_of(j * TK, TK)
        kblk = k_ref[0, 0, pl.ds(start, TK), :]
        st_ref[slot] = _dot(kblk, qt_ref[...])

    def values_t(j):
        return jnp.concatenate([vt_ref[0, 0, j * sub + s] for s in range(sub)], axis=1)

    def online_step(j, slot):
        st = st_ref[slot]
        m_old = m_ref[...]
        m_new = jnp.maximum(m_old, jnp.max(st, axis=0, keepdims=True))
        p = jnp.exp2(st - m_new)
        alpha = jnp.exp2(m_old - m_new)
        l_ref[0:1, :] = alpha * l_ref[0:1, :] + jnp.sum(p, axis=0, keepdims=True)
        acc_ref[...] = alpha * acc_ref[...] + _dot(values_t(j), p.astype(_BF16))
        m_ref[...] = m_new

    def run_bounded():
        l = jnp.zeros(l_ref.shape, _F32)
        acc = jnp.zeros(acc_ref.shape, _F32)
        for j in range(nk):
            p = jnp.exp2(_dot(k_ref[0, 0, j * TK:(j + 1) * TK, :], qt_ref[...]))
            l = l + jnp.sum(p.reshape(TK // 8, 8, p.shape[1]), axis=0)
            acc = acc + _dot(values_t(j), p.astype(_BF16))
        l_ref[...] = l
        acc_ref[...] = acc

    def run_online():
        scores(0, 0)

        def pair(jj, carry):
            j = 2 * jj
            scores(j + 1, 1)
            online_step(j, 0)
            scores(j + 2, 0)
            online_step(j + 1, 1)
            return carry

        lax.fori_loop(0, nk // 2 - 1, pair, 0)
        scores(nk - 1, 1)
        online_step(nk - 2, 0)
        online_step(nk - 1, 1)

    m_ref[...] = jnp.full(m_ref.shape, -jnp.inf, _F32)
    l_ref[...] = jnp.zeros(l_ref.shape, _F32)
    acc_ref[...] = jnp.zeros(acc_ref.shape, _F32)
    lax.cond(bounded_ref[0] == 1, run_bounded, run_online)
    l = jnp.sum(l_ref[...], axis=0, keepdims=True)
    o_ref[0] = (acc_ref[...] / l).T.astype(o_ref.dtype)


def _cast_rider_spec(w, n_steps, step_of):
    rows, cols = w.shape
    units = rows // BF16_SUBLANES
    assert rows % BF16_SUBLANES == 0
    per = next(k for k in range(1, units + 1) if units % k == 0 and units // k <= n_steps)
    n_blocks = units // per
    return pl.BlockSpec((per * BF16_SUBLANES, cols), lambda b, h, i: (jnp.minimum(step_of(b, h, i), n_blocks - 1), 0))


def _attn_call(bounded, q, k, vt, later_weights):
    B, H, S, _ = q.shape
    assert S % (2 * TK) == 0 and S % TQ == 0 and TK % TM_PRE == 0
    nq = S // TQ
    grid = (B, H, nq)
    riders = [_cast_rider_spec(w, B * H * nq, lambda b, h, i: (b * H + h) * nq + i) for w in later_weights]
    return pl.pallas_call(
        _attn_kernel, grid=grid,
        in_specs=[
            pl.BlockSpec(memory_space=pltpu.SMEM),
            pl.BlockSpec((1, 1, TQ, HEAD_SLAB), lambda b, h, i: (b, h, i, 0)),
            pl.BlockSpec((1, 1, S, HEAD_SLAB), lambda b, h, i: (b, h, 0, 0)),
            pl.BlockSpec((1, 1, S // TM_PRE, V_DIM, TM_PRE), lambda b, h, i: (b, h, 0, 0, 0)),
        ] + riders,
        out_specs=[pl.BlockSpec((1, TQ, V_DIM), lambda b, h, i: (b, i, h))] + riders,
        out_shape=[jax.ShapeDtypeStruct((B, S, H * V_DIM), _BF16)]
                  + [jax.ShapeDtypeStruct(w.shape, _BF16) for w in later_weights],
        scratch_shapes=[
            pltpu.VMEM((HEAD_SLAB, TQ), _BF16),
            pltpu.VMEM((2, TK, TQ), _F32),
            pltpu.VMEM((1, TQ), _F32),
            pltpu.VMEM((8, TQ), _F32),
            pltpu.VMEM((V_DIM, TQ), _F32),
        ],
        compiler_params=pltpu.CompilerParams(
            dimension_semantics=("arbitrary", "arbitrary", "arbitrary"), vmem_limit_bytes=V7X_VMEM_LIMIT),
        name="attn",
    )(bounded, q, k, vt, *later_weights)


def _mix_ffn_kernel(x_ref, xp_ref, xn_ref, o_ref, op_ref, on_ref, gya_ref, gyap_ref, gyan_ref,
                    gb_ref, gbp_ref, gbn_ref, wbo_ref, wout_ref, g2_ref, wup_ref, cw_ref, cb_ref, wdown_ref,
                    out_ref, ext_ref, act_ref):
    i = pl.program_id(1)
    nt = pl.num_programs(1)
    tm = x_ref.shape[1]

    def rows3(prev, main, nxt):
        return jnp.concatenate([prev[0], main[0], nxt[0]], axis=0)

    yb = _dot(rows3(op_ref, o_ref, on_ref), wbo_ref[...])
    merged = (rows3(gyap_ref, gya_ref, gyan_ref).astype(_F32)
              + rows3(gbp_ref, gb_ref, gbn_ref).astype(_F32) * yb).astype(_BF16)
    h_ext = rows3(xp_ref, x_ref, xn_ref) + _dot(merged, wout_ref[...])
    ext_ref[...] = _rms(h_ext, g2_ref[...]).astype(_BF16)
    out_ref[0] = h_ext[HALO:HALO + tm]

    keep_prev = (i > 0).astype(_F32)
    keep_next = (i < nt - 1).astype(_F32)

    def up_proj(c):
        ups = []
        for col in (c * FF_CHUNK, D_FF + c * FF_CHUNK):
            up = _dot(ext_ref[...], wup_ref[:, col:col + FF_CHUNK])
            ups.append(jnp.concatenate(
                [up[:HALO] * keep_prev, up[HALO:HALO + tm], up[HALO + tm:] * keep_next], axis=0))
        return ups

    def conv(up, col):
        cols = slice(col, col + FF_CHUNK)
        rows = up.shape[0]
        return (pltpu.roll(up, 1, 0)[HALO:HALO + tm] * cw_ref[0:1, cols]
                + up[HALO:HALO + tm] * cw_ref[1:2, cols]
                + pltpu.roll(up, rows - 1, 0)[HALO:HALO + tm] * cw_ref[2:3, cols] + cb_ref[:, cols])

    n_chunks = D_FF // FF_CHUNK
    up_next = up_proj(0)
    for c in range(n_chunks):
        up_val, up_gate = up_next
        if c + 1 < n_chunks:
            up_next = up_proj(c + 1)
        act_ref[:, c * FF_CHUNK:(c + 1) * FF_CHUNK] = (
            jax.nn.silu(conv(up_gate, D_FF + c * FF_CHUNK)) * conv(up_val, c * FF_CHUNK)).astype(_BF16)
    out_ref[0] += _dot(act_ref[...], wdown_ref[...])


def _mix_ffn_call(x, o, gya, gb, wbo, wout, g2, wup, cw, cb, wdown):
    B, S, D = x.shape
    tm = TM_FFN
    nt = S // tm
    per = tm // HALO
    main = pl.BlockSpec((1, tm, D), lambda b, i: (b, i, 0))
    prev = pl.BlockSpec((1, HALO, D), lambda b, i: (b, jnp.maximum(i * per - 1, 0), 0))
    nxt = pl.BlockSpec((1, HALO, D), lambda b, i: (b, jnp.minimum((i + 1) * per, S // HALO - 1), 0))
    consts = (wbo, wout, g2, wup, cw, cb, wdown)
    return pl.pallas_call(
        _mix_ffn_kernel, grid=(B, nt),
        in_specs=[main, prev, nxt] * 4 + [_const_spec(c.shape) for c in consts],
        out_specs=main,
        out_shape=jax.ShapeDtypeStruct((B, S, D), _F32),
        scratch_shapes=[
            pltpu.VMEM((tm + 2 * HALO, D), _BF16),
            pltpu.VMEM((tm, D_FF), _BF16),
        ],
        compiler_params=pltpu.CompilerParams(
            dimension_semantics=("parallel", "parallel"), vmem_limit_bytes=V7X_VMEM_LIMIT),
        name="mix_ffn",
    )(x, x, x, o, o, o, gya, gya, gya, gb, gb, gb, *consts)


def _pad_head_cols(w):
    lead = w.shape[:-1]
    w = w.reshape(lead + (N_HEADS, QK_DIM))
    w = jnp.pad(w, [(0, 0)] * len(lead) + [(0, 0), (0, HEAD_SLAB - QK_DIM)])
    return w.reshape(lead + (N_HEADS * HEAD_SLAB,))


def kernel(x, positions, norm1_g, w_in, v_ln_g, v_ln_b, w_s, b_s, w_a_o, q_norm_g, w_uq, kv_norm_g, w_ukv,
           q_head_g, k_head_g, w_b_o, w_out, norm2_g, w_up, conv_w, conv_b, w_down):
    B, S, D = x.shape
    bf = lambda a: a.astype(_BF16)
    row = lambda a: a.reshape(1, -1).astype(_F32)

    pad_g = lambda g: jnp.pad(g, (0, HEAD_SLAB - QK_DIM)).reshape(1, HEAD_SLAB).astype(_F32)
    inv_freq = ROPE_THETA ** (-jnp.arange(0, QK_ROPE, 2, dtype=_F32) / QK_ROPE)
    zeros = jnp.zeros((LANES - QK_ROPE,), _F32)
    freq = jnp.tile(inv_freq, ROPE_PACK).reshape(1, LANES)
    sign = jnp.concatenate([-jnp.ones((ROPE_HALF,), _F32), jnp.ones((ROPE_HALF,), _F32), zeros]).reshape(1, LANES)

    bound = (QSCALE_LOG2 * QK_DIM * (1.0 + 2.0 ** -6)) * jnp.max(jnp.abs(q_head_g)) * jnp.max(jnp.abs(k_head_g))
    bounded = bound <= SCORE_BOUND_MAX
    bias = jnp.zeros((2, LANES), _F32).at[0, BIAS_LANE].set(1.0).at[1, BIAS_LANE].set(
        jnp.where(bounded, -bound, 0.0).astype(_F32))

    consts = [
        row(norm1_g), bf(w_in.T),
        row(v_ln_g), row(v_ln_b), bf(w_s), jnp.transpose(b_s).astype(_F32), bf(w_a_o),
        row(q_norm_g), bf(_pad_head_cols(w_uq)), row(kv_norm_g), bf(w_ukv),
        pad_g(q_head_g), pad_g(k_head_g), freq, sign, bias,
    ]
    gya, gb, q, k, vt = _pre_call(x, positions.reshape(B, 1, S), consts)
    o, w_up_bf, w_down_bf, w_b_o_bf, w_out_bf = _attn_call(
        bounded.astype(jnp.int32).reshape(1), q, k, vt, [w_up, w_down, w_b_o, w_out])
    return _mix_ffn_call(x, o, gya, gb, w_b_o_bf, w_out_bf, row(norm2_g),
                         w_up_bf, conv_w.astype(_F32), row(conv_b), w_down_bf)
```

```python
import math

import jax
import jax.numpy as jnp
from jax import lax
from jax.experimental import pallas as pl
from jax.experimental.pallas import tpu as pltpu

D_MODEL = 1024
CHUNK = 128
A_GROUPS = 8
A_GROUP_DIM = 128
D_A = A_GROUPS * A_GROUP_DIM
N_HEADS = 8
Q_RANK = 384
KV_RANK = 256
QK_NOPE = 128
QK_ROPE = 64
V_DIM = 128
QK_DIM = QK_NOPE + QK_ROPE
ROPE_THETA = 10000.0
D_FF = 2816
EPS = 1e-6

IN_V = D_A
IN_CQ = 2 * D_A
IN_CKV = IN_CQ + Q_RANK
IN_KR = IN_CKV + KV_RANK
IN_GA = IN_KR + QK_ROPE
IN_GB = IN_GA + D_MODEL
IN_END = IN_GB + D_MODEL

LANES = 128
BF16_SUBLANES = 16
HEAD_SLAB = 2 * LANES
ROPE_HALF = QK_ROPE // 2
ROPE_PACK = LANES // ROPE_HALF
BIAS_LANE = QK_ROPE
QSCALE_LOG2 = math.log2(math.e) / math.sqrt(QK_DIM)
SCORE_BOUND_MAX = 40.0
V7X_VMEM_LIMIT = 56 * 1024 * 1024

TM_PRE = 512
TQ = 2048
TQ_COLS = 512
TK = 1024
TM_FFN = 512
FF_CHUNK = 256
HALO = 16

_BF16 = jnp.bfloat16
_F32 = jnp.float32


def _dot(a, b):
    return jnp.dot(a, b, preferred_element_type=_F32)


def _gelu(x):
    c = 2.0 * math.sqrt(2.0 / math.pi) * math.log2(math.e)
    z = x * (x * x * (-0.044715 * c) - c)
    return x / (1.0 + jnp.exp2(z))


def _rms(x, g):
    return x * lax.rsqrt(jnp.mean(x * x, axis=-1, keepdims=True) + EPS) * g


def _rope_partner(t):
    lane = lax.broadcasted_iota(jnp.int32, t.shape, 1)
    return jnp.where(lane < ROPE_HALF, pltpu.roll(t, LANES - ROPE_HALF, 1), pltpu.roll(t, ROPE_HALF, 1))


def _const_spec(shape):
    nd = len(shape)
    return pl.BlockSpec(shape, lambda *_: (0,) * nd, pipeline_mode=pl.Buffered(1))


def _pre_kernel(x_ref, pos_ref, g1_ref, wint_ref,
                lng_ref, lnb_ref, ws_ref, bs_ref, wao_ref, qng_ref, wuq_ref, kvng_ref, wukv_ref,
                qg_ref, kg_ref, freq_ref, sign_ref, bias_ref,
                gya_ref, gb_ref, q_ref, k_ref, vt_ref):
    tm = x_ref.shape[1]
    xn = _rms(x_ref[0], g1_ref[...]).astype(_BF16)

    def in_proj(lo, hi):
        return lax.dot_general(xn, wint_ref[lo:hi, :], (((1,), (1,)), ((), ())), preferred_element_type=_F32)

    cq = in_proj(IN_CQ, IN_CKV)
    ckv = in_proj(IN_CKV, IN_KR)
    kpe_tile = in_proj(IN_KR, IN_KR + LANES)
    kpe = jnp.where(lax.broadcasted_iota(jnp.int32, kpe_tile.shape, 1) < QK_ROPE, kpe_tile, 0.0)
    v_raw = in_proj(IN_V, IN_CQ)
    qall = _dot(_rms(cq, qng_ref[...]).astype(_BF16), wuq_ref[...])
    kv = _dot(_rms(ckv, kvng_ref[...]).astype(_BF16), wukv_ref[...])
    gb_ref[0] = jax.nn.sigmoid(in_proj(IN_GB, IN_END)).astype(gb_ref.dtype)
    u_raw = in_proj(0, IN_V)
    ga_raw = in_proj(IN_GA, IN_GB)

    gv = _gelu(v_raw)
    mu = jnp.mean(gv, axis=-1, keepdims=True)
    vc = gv - mu
    vln = (vc * lax.rsqrt(jnp.mean(vc * vc, axis=-1, keepdims=True) + EPS) * lng_ref[...]
           + lnb_ref[...]).astype(_BF16)
    rows = []
    for c in range(tm // CHUNK):
        cols = []
        for g in range(A_GROUPS):
            blk = vln[c * CHUNK:(c + 1) * CHUNK, g * A_GROUP_DIM:(g + 1) * A_GROUP_DIM]
            cols.append(_dot(ws_ref[g], blk) + bs_ref[:, g:g + 1])
        rows.append(jnp.concatenate(cols, axis=1))
    mixed = jnp.concatenate(rows, axis=0)
    y = (_gelu(u_raw) * mixed).astype(_BF16)
    gya_ref[0] = (jax.nn.sigmoid(ga_raw) * _dot(y, wao_ref[...])).astype(gya_ref.dtype)

    pos_col = jnp.broadcast_to(pos_ref[0].astype(_F32), (LANES, tm)).T
    rg = tm // ROPE_PACK
    lane = lax.broadcasted_iota(jnp.int32, (rg, LANES), 1)
    pos_packed = pos_col[(ROPE_PACK - 1) * rg:]
    for g in range(ROPE_PACK - 2, -1, -1):
        pos_packed = jnp.where(lane < (g + 1) * ROPE_HALF, pos_col[g * rg:(g + 1) * rg], pos_packed)
    ang = pos_packed * freq_ref[...]
    cos_p, sin_p = jnp.cos(ang), jnp.sin(ang)

    def unpack(tab):
        tiles = []
        shifted = lambda s: tab if s % LANES == 0 else pltpu.roll(tab, s % LANES, 1)
        for g in range(ROPE_PACK):
            lo = shifted(-g * ROPE_HALF)
            hi = shifted((1 - g) * ROPE_HALF)
            tiles.append(jnp.where(lane < ROPE_HALF, lo, hi))
        return jnp.concatenate(tiles, axis=0)

    cos_t = unpack(cos_p)
    sin_t = unpack(sin_p) * sign_ref[...]

    def rope(t):
        return t * cos_t + _rope_partner(t) * sin_t

    for h in range(N_HEADS):
        qh = qall[:, h * HEAD_SLAB:(h + 1) * HEAD_SLAB]
        r = lax.rsqrt(jnp.sum(qh * qh, axis=-1, keepdims=True) * (1.0 / QK_DIM) + EPS) * QSCALE_LOG2
        qn = qh * r * qg_ref[...]
        q_ref[0, h] = jnp.concatenate([qn[:, :LANES], rope(qn[:, LANES:]) + bias_ref[0:1, :]],
                                      axis=1).astype(q_ref.dtype)

    kpe_ss = jnp.sum(kpe * kpe, axis=-1, keepdims=True)
    kpe_rot = rope(kpe * kg_ref[:, LANES:])
    for h in range(N_HEADS):
        kn = kv[:, h * HEAD_SLAB:h * HEAD_SLAB + LANES]
        r = lax.rsqrt((jnp.sum(kn * kn, axis=-1, keepdims=True) + kpe_ss) * (1.0 / QK_DIM) + EPS)
        k_ref[0, h] = jnp.concatenate([kn * r * kg_ref[:, :LANES], kpe_rot * r + bias_ref[1:2, :]],
                                      axis=1).astype(k_ref.dtype)
        vt_ref[0, h, 0] = kv[:, h * HEAD_SLAB + LANES:(h + 1) * HEAD_SLAB].T.astype(vt_ref.dtype)


def _pre_call(x, pos3, consts):
    B, S, D = x.shape
    tm = TM_PRE
    nt = S // tm
    grid = (B, nt)
    in_specs = [
        pl.BlockSpec((1, tm, D), lambda b, i: (b, i, 0)),
        pl.BlockSpec((1, 1, tm), lambda b, i: (b, 0, i)),
    ] + [_const_spec(c.shape) for c in consts]
    out_shape = [
        jax.ShapeDtypeStruct((B, S, D), _BF16),
        jax.ShapeDtypeStruct((B, S, D), _BF16),
        jax.ShapeDtypeStruct((B, N_HEADS, S, HEAD_SLAB), _BF16),
        jax.ShapeDtypeStruct((B, N_HEADS, S, HEAD_SLAB), _BF16),
        jax.ShapeDtypeStruct((B, N_HEADS, nt, V_DIM, tm), _BF16),
    ]
    out_specs = [
        pl.BlockSpec((1, tm, D), lambda b, i: (b, i, 0)),
        pl.BlockSpec((1, tm, D), lambda b, i: (b, i, 0)),
        pl.BlockSpec((1, N_HEADS, tm, HEAD_SLAB), lambda b, i: (b, 0, i, 0)),
        pl.BlockSpec((1, N_HEADS, tm, HEAD_SLAB), lambda b, i: (b, 0, i, 0)),
        pl.BlockSpec((1, N_HEADS, 1, V_DIM, tm), lambda b, i: (b, 0, i, 0, 0)),
    ]
    return pl.pallas_call(
        _pre_kernel, grid=grid, in_specs=in_specs, out_specs=out_specs, out_shape=out_shape,
        compiler_params=pltpu.CompilerParams(
            dimension_semantics=("parallel", "parallel"), vmem_limit_bytes=V7X_VMEM_LIMIT),
        name="pre",
    )(x, pos3, *consts)


def _attn_kernel(bounded_ref, q_ref, k_ref, vt_ref, wup_ref, wdown_ref, wbo_ref, wout_ref,
                 o_ref, wup_bf_ref, wdown_bf_ref, wbo_bf_ref, wout_bf_ref,
                 qt_ref, st_ref, m_ref, l_ref, acc_ref):
    for src, dst in ((wup_ref, wup_bf_ref), (wdown_ref, wdown_bf_ref), (wbo_ref, wbo_bf_ref),
                     (wout_ref, wout_bf_ref)):
        dst[...] = src[...].astype(dst.dtype)

    sub = TK // TM_PRE
    nk = k_ref.shape[2] // TK
    qt_ref[...] = q_ref[0, 0].T

    def scores(j, slot):
        start = j * TK if isinstance(j, int) else pl.multiple_of(j * TK, TK)
        kblk = k_ref[0, 0, pl.ds(start, TK), :]
        st_ref[slot] = _dot(kblk, qt_ref[...])

    def values_t(j):
        return jnp.concatenate([vt_ref[0, 0, j * sub + s] for s in range(sub)], axis=1)

    def online_step(j, slot):
        st = st_ref[slot]
        m_old = m_ref[...]
        m_new = jnp.maximum(m_old, jnp.max(st, axis=0, keepdims=True))
        p = jnp.exp2(st - m_new)
        alpha = jnp.exp2(m_old - m_new)
        l_ref[0:1, :] = alpha * l_ref[0:1, :] + jnp.sum(p, axis=0, keepdims=True)
        acc_ref[...] = alpha * acc_ref[...] + _dot(values_t(j), p.astype(_BF16))
        m_ref[...] = m_new

    def run_bounded():
        n_keys = k_ref.shape[2]
        vt_all = jnp.concatenate([values_t(j) for j in range(nk)], axis=1)
        for t in range(qt_ref.shape[1] // TQ_COLS):
            cols = slice(t * TQ_COLS, (t + 1) * TQ_COLS)
            p = jnp.exp2(_dot(k_ref[0, 0], qt_ref[:, cols]))
            l_ref[:, cols] = jnp.sum(p.reshape(n_keys // 8, 8, TQ_COLS), axis=0)
            acc_ref[:, cols] = _dot(vt_all, p.astype(_BF16))

    def run_online():
        scores(0, 0)

        def pair(jj, carry):
            j = 2 * jj
            scores(j + 1, 1)
            online_step(j, 0)
            scores(j + 2, 0)
            online_step(j + 1, 1)
            return carry

        lax.fori_loop(0, nk // 2 - 1, pair, 0)
        scores(nk - 1, 1)
        online_step(nk - 2, 0)
        online_step(nk - 1, 1)

    m_ref[...] = jnp.full(m_ref.shape, -jnp.inf, _F32)
    l_ref[...] = jnp.zeros(l_ref.shape, _F32)
    acc_ref[...] = jnp.zeros(acc_ref.shape, _F32)
    lax.cond(bounded_ref[0] == 1, run_bounded, run_online)
    l = jnp.sum(l_ref[...], axis=0, keepdims=True)
    o_ref[0] = (acc_ref[...] / l).T.astype(o_ref.dtype)


def _cast_rider_spec(w, n_steps, step_of):
    rows, cols = w.shape
    units = rows // BF16_SUBLANES
    assert rows % BF16_SUBLANES == 0
    per = next(k for k in range(1, units + 1) if units % k == 0 and units // k <= n_steps)
    n_blocks = units // per
    return pl.BlockSpec((per * BF16_SUBLANES, cols), lambda b, h, i: (jnp.minimum(step_of(b, h, i), n_blocks - 1), 0))


def _attn_call(bounded, q, k, vt, later_weights):
    B, H, S, _ = q.shape
    assert S % (2 * TK) == 0 and S % TQ == 0 and TK % TM_PRE == 0
    nq = S // TQ
    grid = (B, H, nq)
    riders = [_cast_rider_spec(w, B * H * nq, lambda b, h, i: (b * H + h) * nq + i) for w in later_weights]
    return pl.pallas_call(
        _attn_kernel, grid=grid,
        in_specs=[
            pl.BlockSpec(memory_space=pltpu.SMEM),
            pl.BlockSpec((1, 1, TQ, HEAD_SLAB), lambda b, h, i: (b, h, i, 0)),
            pl.BlockSpec((1, 1, S, HEAD_SLAB), lambda b, h, i: (b, h, 0, 0)),
            pl.BlockSpec((1, 1, S // TM_PRE, V_DIM, TM_PRE), lambda b, h, i: (b, h, 0, 0, 0)),
        ] + riders,
        out_specs=[pl.BlockSpec((1, TQ, V_DIM), lambda b, h, i: (b, i, h))] + riders,
        out_shape=[jax.ShapeDtypeStruct((B, S, H * V_DIM), _BF16)]
                  + [jax.ShapeDtypeStruct(w.shape, _BF16) for w in later_weights],
        scratch_shapes=[
            pltpu.VMEM((HEAD_SLAB, TQ), _BF16),
            pltpu.VMEM((2, TK, TQ), _F32),
            pltpu.VMEM((1, TQ), _F32),
            pltpu.VMEM((8, TQ), _F32),
            pltpu.VMEM((V_DIM, TQ), _F32),
        ],
        compiler_params=pltpu.CompilerParams(
            dimension_semantics=("arbitrary", "arbitrary", "arbitrary"), vmem_limit_bytes=V7X_VMEM_LIMIT),
        name="attn",
    )(bounded, q, k, vt, *later_weights)


def _mix_ffn_kernel(x_ref, xp_ref, xn_ref, o_ref, op_ref, on_ref, gya_ref, gyap_ref, gyan_ref,
                    gb_ref, gbp_ref, gbn_ref, wbo_ref, wout_ref, g2_ref, wup_ref, cw_ref, cb_ref, wdown_ref,
                    out_ref, ext_ref, act_ref):
    i = pl.program_id(1)
    nt = pl.num_programs(1)
    tm = x_ref.shape[1]

    def rows3(prev, main, nxt):
        return jnp.concatenate([prev[0], main[0], nxt[0]], axis=0)

    yb = _dot(rows3(op_ref, o_ref, on_ref), wbo_ref[...])
    merged = (rows3(gyap_ref, gya_ref, gyan_ref).astype(_F32)
              + rows3(gbp_ref, gb_ref, gbn_ref).astype(_F32) * yb).astype(_BF16)
    h_ext = rows3(xp_ref, x_ref, xn_ref) + _dot(merged, wout_ref[...])
    ext_ref[...] = _rms(h_ext, g2_ref[...]).astype(_BF16)
    out_ref[0] = h_ext[HALO:HALO + tm]

    keep_prev = (i > 0).astype(_F32)
    keep_next = (i < nt - 1).astype(_F32)

    def up_proj(c):
        ups = []
        for col in (c * FF_CHUNK, D_FF + c * FF_CHUNK):
            up = _dot(ext_ref[...], wup_ref[:, col:col + FF_CHUNK])
            ups.append(jnp.concatenate(
                [up[:HALO] * keep_prev, up[HALO:HALO + tm], up[HALO + tm:] * keep_next], axis=0))
        return ups

    def conv(up, col):
        cols = slice(col, col + FF_CHUNK)
        rows = up.shape[0]
        return (pltpu.roll(up, 1, 0)[HALO:HALO + tm] * cw_ref[0:1, cols]
                + up[HALO:HALO + tm] * cw_ref[1:2, cols]
                + pltpu.roll(up, rows - 1, 0)[HALO:HALO + tm] * cw_ref[2:3, cols] + cb_ref[:, cols])

    n_chunks = D_FF // FF_CHUNK
    up_next = up_proj(0)
    for c in range(n_chunks):
        up_val, up_gate = up_next
        if c + 1 < n_chunks:
            up_next = up_proj(c + 1)
        act_ref[:, c * FF_CHUNK:(c + 1) * FF_CHUNK] = (
            jax.nn.silu(conv(up_gate, D_FF + c * FF_CHUNK)) * conv(up_val, c * FF_CHUNK)).astype(_BF16)
    out_ref[0] += _dot(act_ref[...], wdown_ref[...])


def _mix_ffn_call(x, o, gya, gb, wbo, wout, g2, wup, cw, cb, wdown):
    B, S, D = x.shape
    tm = TM_FFN
    nt = S // tm
    per = tm // HALO
    main = pl.BlockSpec((1, tm, D), lambda b, i: (b, i, 0))
    prev = pl.BlockSpec((1, HALO, D), lambda b, i: (b, jnp.maximum(i * per - 1, 0), 0))
    nxt = pl.BlockSpec((1, HALO, D), lambda b, i: (b, jnp.minimum((i + 1) * per, S // HALO - 1), 0))
    consts = (wbo, wout, g2, wup, cw, cb, wdown)
    return pl.pallas_call(
        _mix_ffn_kernel, grid=(B, nt),
        in_specs=[main, prev, nxt] * 4 + [_const_spec(c.shape) for c in consts],
        out_specs=main,
        out_shape=jax.ShapeDtypeStruct((B, S, D), _F32),
        scratch_shapes=[
            pltpu.VMEM((tm + 2 * HALO, D), _BF16),
            pltpu.VMEM((tm, D_FF), _BF16),
        ],
        compiler_params=pltpu.CompilerParams(
            dimension_semantics=("parallel", "parallel"), vmem_limit_bytes=V7X_VMEM_LIMIT),
        name="mix_ffn",
    )(x, x, x, o, o, o, gya, gya, gya, gb, gb, gb, *consts)


def _pad_head_cols(w):
    lead = w.shape[:-1]
    w = w.reshape(lead + (N_HEADS, QK_DIM))
    w = jnp.pad(w, [(0, 0)] * len(lead) + [(0, 0), (0, HEAD_SLAB - QK_DIM)])
    return w.reshape(lead + (N_HEADS * HEAD_SLAB,))


def kernel(x, positions, norm1_g, w_in, v_ln_g, v_ln_b, w_s, b_s, w_a_o, q_norm_g, w_uq, kv_norm_g, w_ukv,
           q_head_g, k_head_g, w_b_o, w_out, norm2_g, w_up, conv_w, conv_b, w_down):
    B, S, D = x.shape
    bf = lambda a: a.astype(_BF16)
    row = lambda a: a.reshape(1, -1).astype(_F32)

    pad_g = lambda g: jnp.pad(g, (0, HEAD_SLAB - QK_DIM)).reshape(1, HEAD_SLAB).astype(_F32)
    inv_freq = ROPE_THETA ** (-jnp.arange(0, QK_ROPE, 2, dtype=_F32) / QK_ROPE)
    zeros = jnp.zeros((LANES - QK_ROPE,), _F32)
    freq = jnp.tile(inv_freq, ROPE_PACK).reshape(1, LANES)
    sign = jnp.concatenate([-jnp.ones((ROPE_HALF,), _F32), jnp.ones((ROPE_HALF,), _F32), zeros]).reshape(1, LANES)

    bound = (QSCALE_LOG2 * QK_DIM * (1.0 + 2.0 ** -6)) * jnp.max(jnp.abs(q_head_g)) * jnp.max(jnp.abs(k_head_g))
    bounded = bound <= SCORE_BOUND_MAX
    bias = jnp.zeros((2, LANES), _F32).at[0, BIAS_LANE].set(1.0).at[1, BIAS_LANE].set(
        jnp.where(bounded, -bound, 0.0).astype(_F32))

    consts = [
        row(norm1_g), bf(w_in.T),
        row(v_ln_g), row(v_ln_b), bf(w_s), jnp.transpose(b_s).astype(_F32), bf(w_a_o),
        row(q_norm_g), bf(_pad_head_cols(w_uq)), row(kv_norm_g), bf(w_ukv),
        pad_g(q_head_g), pad_g(k_head_g), freq, sign, bias,
    ]
    gya, gb, q, k, vt = _pre_call(x, positions.reshape(B, 1, S), consts)
    o, w_up_bf, w_down_bf, w_b_o_bf, w_out_bf = _attn_call(
        bounded.astype(jnp.int32).reshape(1), q, k, vt, [w_up, w_down, w_b_o, w_out])
    return _mix_ffn_call(x, o, gya, gb, w_b_o_bf, w_out_bf, row(norm2_g),
                         w_up_bf, conv_w.astype(_F32), row(conv_b), w_down_bf)
```
